```python
import math
import functools
import jax
import jax.numpy as jnp
from jax import lax
import numpy as np

D_MODEL = 2048
BATCH = 4
SEQ = 4096
DEPTH = 2
DEC_BATCH = 128
DEC_SEQ = 4
PAST_LEN = 16384
PAGE_SIZE = 128

GLA_HEADS = 4
GLA_DK = 128
GLA_DV = 256
GLA_RANK = 16
GLA_TAU = 16.0
GLA_CHUNK = 64
GLA_QK = GLA_HEADS * GLA_DK
GLA_VW = GLA_HEADS * GLA_DV
SSM_WIDTH = 1024
SSM_GROUP = 16
SSM_GROUPS = SSM_WIDTH // SSM_GROUP
SSM_STATE = 64
DT_MIN = 1e-3
DT_MAX = 1e-1
MLA_HEADS = 8
Q_LORA = 512
KV_LORA = 256
NOPE_DIM = 128
ROPE_DIM = 64
V_DIM = 128
MLA_VW = MLA_HEADS * V_DIM
ROPE_THETA = 10000.0
Q_BLOCK = 128
ATTN_SCALE = (NOPE_DIM + ROPE_DIM) ** -0.5
N_EXPERTS = 16
N_EXPERT_GROUPS = 4
EXPERTS_PER_GROUP = N_EXPERTS // N_EXPERT_GROUPS
TOP_K = 2
D_EXPERT = 1024
N_BRANCH = 3
N_MOD = 6
RMS_EPS = 1e-6
IN_SPLITS = (GLA_QK, GLA_QK, GLA_VW, GLA_VW, GLA_RANK, SSM_WIDTH, Q_LORA, KV_LORA, ROPE_DIM, N_BRANCH * D_MODEL)
D_IN = sum(IN_SPLITS)

kernel_name = 'hybrid_gla_s5_mla_moe_step'

F32 = jnp.float32


def rmsnorm(x, g):
    xf = x.astype(F32)
    y = xf * lax.rsqrt(jnp.mean(xf * xf, axis=-1, keepdims=True) + RMS_EPS)
    return (y * g.astype(F32)).astype(x.dtype)


def apply_rope(x, pos):
    half = ROPE_DIM // 2
    inv = ROPE_THETA ** (-jnp.arange(half, dtype=F32) / half)
    ang = pos.astype(F32)[:, None] * inv[None, :]
    ang = ang.reshape((ang.shape[0],) + (1,) * (x.ndim - 3) + (half,))
    cos, sin = jnp.cos(ang), jnp.sin(ang)
    xf = x.astype(F32)
    x1, x2 = xf[..., :half], xf[..., half:]
    return jnp.concatenate([x1 * cos - x2 * sin, x1 * sin + x2 * cos], axis=-1).astype(x.dtype)


def gla_recurrence(q, k, v, log_a, s0):
    B, T, H, DK = q.shape
    DV = v.shape[-1]
    C = math.gcd(T, GLA_CHUNK)
    n = T // C

    def chunks(a):
        return a.reshape((B, n, C) + a.shape[2:]).swapaxes(0, 1)

    causal = jnp.tril(jnp.ones((C, C), dtype=bool))[None, :, :, None, None]

    def step(S, inp):
        qc, kc, vc, ac = inp
        b = jnp.cumsum(ac, axis=1)
        decay = jnp.exp(jnp.where(causal, b[:, :, None] - b[:, None, :], -jnp.inf))
        scores = jnp.einsum('bihk,bjhk,bijhk->bhij', qc, kc, decay)
        o = (jnp.einsum('bhij,bjhv->bihv', scores, vc)
             + jnp.einsum('bihk,bhkv->bihv', qc * jnp.exp(b), S))
        b_last = b[:, -1]
        S = (jnp.exp(b_last)[..., None] * S
             + jnp.einsum('bjhk,bjhv->bhkv', kc * jnp.exp(b_last[:, None] - b), vc))
        return S, o

    S, o = lax.scan(step, s0, (chunks(q), chunks(k), chunks(v), chunks(log_a)))
    return o.swapaxes(0, 1).reshape(B, T, H, DV), S


def gla_branch(q, k, v, g, a_lr, w_a2, b_a2, norm_g, s0):
    B, T, _ = q.shape
    log_a = jax.nn.log_sigmoid(a_lr.astype(F32) @ w_a2.astype(F32) + b_a2.astype(F32)) / GLA_TAU
    qh = q.astype(F32).reshape(B, T, GLA_HEADS, GLA_DK) * (GLA_DK ** -0.5)
    kh = k.astype(F32).reshape(B, T, GLA_HEADS, GLA_DK)
    vh = v.astype(F32).reshape(B, T, GLA_HEADS, GLA_DV)
    o, s = gla_recurrence(qh, kh, vh, log_a.reshape(B, T, GLA_HEADS, GLA_DK), s0.astype(F32))
    o = rmsnorm(o, norm_g).reshape(B, T, GLA_VW) * jax.nn.silu(g.astype(F32))
    return o.astype(q.dtype), s


def _complex_linear_combine(e1, e2):
    a1r, a1i, b1r, b1i = e1
    a2r, a2i, b2r, b2i = e2
    return (a2r * a1r - a2i * a1i,
            a2r * a1i + a2i * a1r,
            a2r * b1r - a2i * b1i + b2r,
            a2r * b1i + a2i * b1r + b2i)


def s5_branch(u, lam_re, lam_im, log_dt, b_re, b_im, c_re, c_im, d_skip, w_glu, b_glu, x0_re, x0_im):
    B, T, _ = u.shape
    uf = u.astype(F32).reshape(B, T, SSM_GROUPS, SSM_GROUP)
    dt = jnp.exp(log_dt.astype(F32))[:, None]
    lr, li = lam_re.astype(F32), lam_im.astype(F32)
    mag, ang = jnp.exp(lr * dt), li * dt
    ab_re, ab_im = mag * jnp.cos(ang), mag * jnp.sin(ang)
    den = lr * lr + li * li
    f_re = ((ab_re - 1.0) * lr + ab_im * li) / den
    f_im = (ab_im * lr - (ab_re - 1.0) * li) / den
    br, bi = b_re.astype(F32), b_im.astype(F32)
    bb_re = f_re[..., None] * br - f_im[..., None] * bi
    bb_im = f_re[..., None] * bi + f_im[..., None] * br
    bu_re = jnp.einsum('gps,btgs->btgp', bb_re, uf)
    bu_im = jnp.einsum('gps,btgs->btgp', bb_im, uf)
    x0r, x0i = x0_re.astype(F32), x0_im.astype(F32)
    bu_re = bu_re.at[:, 0].add(ab_re * x0r - ab_im * x0i)
    bu_im = bu_im.at[:, 0].add(ab_re * x0i + ab_im * x0r)
    a_re = jnp.broadcast_to(ab_re, (1, T) + ab_re.shape)
    a_im = jnp.broadcast_to(ab_im, (1, T) + ab_im.shape)
    _, _, xr, xi = lax.associative_scan(_complex_linear_combine, (a_re, a_im, bu_re, bu_im), axis=1)
    y = (jnp.einsum('gsp,btgp->btgs', c_re.astype(F32), xr)
         - jnp.einsum('gsp,btgp->btgs', c_im.astype(F32), xi))
    y = (y + d_skip.astype(F32).reshape(SSM_GROUPS, SSM_GROUP) * uf).reshape(B, T, SSM_WIDTH)
    z = jax.nn.gelu(y)
    out = z * jax.nn.sigmoid(z @ w_glu.astype(F32) + b_glu.astype(F32))
    return out.astype(u.dtype), xr[:, -1], xi[:, -1]


def mla_attend_prompt(q_lat, q_rope, ckv, kr):
    B, T, H, _ = q_lat.shape
    nb = T // Q_BLOCK
    ckv32, kr32 = ckv.astype(F32), kr.astype(F32)
    key_pos = jnp.arange(T)

    def block(i):
        ql = lax.dynamic_slice_in_dim(q_lat, i * Q_BLOCK, Q_BLOCK, axis=1).astype(F32)
        qr = lax.dynamic_slice_in_dim(q_rope, i * Q_BLOCK, Q_BLOCK, axis=1).astype(F32)
        s = (jnp.einsum('bqhc,bsc->bhqs', ql, ckv32)
             + jnp.einsum('bqhr,bsr->bhqs', qr, kr32)) * ATTN_SCALE
        qpos = i * Q_BLOCK + jnp.arange(Q_BLOCK)
        s = jnp.where(key_pos[None, :] <= qpos[:, None], s, -jnp.inf)
        p = jax.nn.softmax(s, axis=-1)
        return jnp.einsum('bhqs,bsc->bqhc', p, ckv32)

    o = lax.map(block, jnp.arange(nb))
    return o.transpose(1, 0, 2, 3, 4).reshape(B, T, H, KV_LORA)


def mla_attend_sample(q_lat, q_rope, ckv_new, kr_new, cache_ckv, cache_krope, page_table, layer):
    Td = q_lat.shape[1]
    causal = jnp.tril(jnp.ones((Td, Td), dtype=bool))

    def one_sequence(args):
        ql, qr, cn, rn, pages = args
        ql, qr, cn, rn = ql.astype(F32), qr.astype(F32), cn.astype(F32), rn.astype(F32)
        cp = cache_ckv[layer, pages].reshape(-1, KV_LORA).astype(F32)
        rp = cache_krope[layer, pages].reshape(-1, ROPE_DIM).astype(F32)
        s_past = jnp.einsum('qhc,sc->hqs', ql, cp) + jnp.einsum('qhr,sr->hqs', qr, rp)
        s_new = jnp.einsum('qhc,sc->hqs', ql, cn) + jnp.einsum('qhr,sr->hqs', qr, rn)
        s_new = jnp.where(causal[None], s_new, -jnp.inf)
        n_past = cp.shape[0]
        p = jax.nn.softmax(jnp.concatenate([s_past, s_new], axis=-1) * ATTN_SCALE, axis=-1)
        return (jnp.einsum('hqs,sc->qhc', p[..., :n_past], cp)
                + jnp.einsum('hqs,sc->qhc', p[..., n_past:], cn))

    return lax.map(one_sequence, (q_lat, q_rope, ckv_new, kr_new, page_table))


def token_mixers(h, pos, lw, gla_s0, ssm_re0, ssm_im0, attend):
    B, T, _ = h.shape
    proj = h @ lw['w_in']
    offsets = np.cumsum(IN_SPLITS)[:-1].tolist()
    q, k, v, g, a_lr, u, cq, ckv, kr, gates = jnp.split(proj, offsets, axis=-1)
    o_gla, gla_s = gla_branch(q, k, v, g, a_lr, lw['gla_w_a2'], lw['gla_b_a2'], lw['gla_norm'], gla_s0)
    o_ssm, ssm_re, ssm_im = s5_branch(u, lw['ssm_lambda_re'], lw['ssm_lambda_im'], lw['ssm_log_dt'],
                                      lw['ssm_b_re'], lw['ssm_b_im'], lw['ssm_c_re'], lw['ssm_c_im'],
                                      lw['ssm_d'], lw['ssm_w_glu'], lw['ssm_b_glu'], ssm_re0, ssm_im0)
    cq = rmsnorm(cq, lw['mla_q_norm'])
    ckv = rmsnorm(ckv, lw['mla_kv_norm'])
    qh = (cq @ lw['mla_w_uq']).reshape(B, T, MLA_HEADS, NOPE_DIM + ROPE_DIM)
    q_nope = qh[..., :NOPE_DIM]
    q_rope = apply_rope(qh[..., NOPE_DIM:], pos)
    kr = apply_rope(kr, pos)
    q_lat = jnp.einsum('bthn,chn->bthc', q_nope, lw['mla_w_uk'])
    o_lat = attend(q_lat, q_rope, ckv, kr).astype(h.dtype)
    o_mla = jnp.einsum('bthc,chv->bthv', o_lat, lw['mla_w_uv']).reshape(B, T, MLA_VW)
    gt = jax.nn.sigmoid(gates.astype(F32)).astype(h.dtype).reshape(B, T, N_BRANCH, D_MODEL)
    merged = (gt[:, :, 0] * (o_gla @ lw['w_br_gla'])
              + gt[:, :, 1] * (o_ssm @ lw['w_br_ssm'])
              + gt[:, :, 2] * (o_mla @ lw['w_br_mla']))
    return merged @ lw['w_out'], (ckv, kr, gla_s, ssm_re, ssm_im)


def moe(h, router_w, router_bias, w_gu, w_down):
    B, T, D = h.shape
    hf = h.reshape(-1, D)
    scores = jax.nn.sigmoid((hf @ router_w).astype(F32))
    sel = scores + router_bias.astype(F32)
    grp = sel.reshape(-1, N_EXPERT_GROUPS, EXPERTS_PER_GROUP)
    grp_score = lax.top_k(grp, TOP_K)[0].sum(-1)
    best = jnp.argmax(grp_score, axis=-1)
    mask = jnp.repeat(jax.nn.one_hot(best, N_EXPERT_GROUPS, dtype=F32) > 0, EXPERTS_PER_GROUP, axis=-1)
    _, idx = lax.top_k(jnp.where(mask, sel, -jnp.inf), TOP_K)
    w = jnp.take_along_axis(scores, idx, axis=-1)
    w = w / jnp.sum(w, axis=-1, keepdims=True)
    gates = jnp.sum(jax.nn.one_hot(idx, N_EXPERTS, dtype=F32) * w[..., None], axis=1)

    def expert(acc, inp):
        wgu, wd, gexp = inp
        a, b = jnp.split(hf @ wgu, 2, axis=-1)
        y = (jax.nn.silu(a) * b) @ wd
        return acc + (gexp[:, None] * y.astype(F32)).astype(acc.dtype), None

    out, _ = lax.scan(expert, jnp.zeros_like(hf), (w_gu, w_down, gates.T))
    return out.reshape(B, T, D)


def trunk_layer(x, c, pos, lw, router_w, router_bias, gla_s0, ssm_re0, ssm_im0, attend):
    mod = (jax.nn.silu(c) @ lw['w_ada'] + lw['b_ada'])[:, None, :]
    sh_m, sc_m, g_m, sh_f, sc_f, g_f = jnp.split(mod, N_MOD, axis=-1)
    h = rmsnorm(x, lw['norm_mix']) * (1.0 + sc_m) + sh_m
    mix, new_state = token_mixers(h, pos, lw, gla_s0, ssm_re0, ssm_im0, attend)
    x = x + g_m * mix
    h = rmsnorm(x, lw['norm_ffn']) * (1.0 + sc_f) + sh_f
    x = x + g_f * moe(h, router_w, router_bias, lw['moe_w_gu'], lw['moe_w_down'])
    return x, new_state


def setup_inputs(seed: int = 0) -> dict:
    key = jax.random.key(seed)
    keys = iter(list(jax.random.split(key, 64)))

    def normal(shape, scale):
        return jax.random.normal(next(keys), shape, F32) * scale

    def gain(shape):
        return 1.0 + normal(shape, 0.02)

    n_pages = PAST_LEN // PAGE_SIZE
    n_pool = (DEC_BATCH * n_pages * 5) // 4
    perm = jax.random.permutation(next(keys), n_pool)
    page_table = perm[: DEC_BATCH * n_pages].reshape(DEC_BATCH, n_pages).astype(jnp.int32)
    lam_im0 = jnp.pi * jnp.arange(SSM_STATE, dtype=F32)
    return {
        'x_prompt': normal((BATCH, SEQ, D_MODEL), 1.0),
        'x_sample': normal((DEC_BATCH, DEC_SEQ, D_MODEL), 1.0),
        'cache_ckv': normal((DEPTH, n_pool, PAGE_SIZE, KV_LORA), 1.0),
        'cache_krope': normal((DEPTH, n_pool, PAGE_SIZE, ROPE_DIM), 1.0),
        'state_gla': normal((DEPTH, DEC_BATCH, GLA_HEADS, GLA_DK, GLA_DV), 1.0),
        'state_ssm_re': normal((DEPTH, DEC_BATCH, SSM_GROUPS, SSM_STATE), 0.5),
        'state_ssm_im': normal((DEPTH, DEC_BATCH, SSM_GROUPS, SSM_STATE), 0.5),
        'page_table': page_table,
        'c_prompt': normal((BATCH, D_MODEL), 1.0),
        'c_sample': normal((DEC_BATCH, D_MODEL), 1.0),
        'norm_mix': gain((DEPTH, D_MODEL)),
        'norm_ffn': gain((DEPTH, D_MODEL)),
        'norm_final': gain((D_MODEL,)),
        'w_ada': normal((DEPTH, D_MODEL, N_MOD * D_MODEL), 0.5 * D_MODEL ** -0.5),
        'b_ada': normal((DEPTH, N_MOD * D_MODEL), 0.02),
        'w_in': normal((DEPTH, D_MODEL, D_IN), D_MODEL ** -0.5),
        'gla_w_a2': normal((DEPTH, GLA_RANK, GLA_QK), GLA_RANK ** -0.5),
        'gla_b_a2': normal((DEPTH, GLA_QK), 0.1),
        'gla_norm': gain((DEPTH, GLA_DV)),
        'ssm_lambda_re': -0.5 + normal((DEPTH, SSM_GROUPS, SSM_STATE), 0.01),
        'ssm_lambda_im': lam_im0 + normal((DEPTH, SSM_GROUPS, SSM_STATE), 0.01),
        'ssm_log_dt': jax.random.uniform(next(keys), (DEPTH, SSM_GROUPS), F32, math.log(DT_MIN), math.log(DT_MAX)),
        'ssm_b_re': normal((DEPTH, SSM_GROUPS, SSM_STATE, SSM_GROUP), (2 * SSM_GROUP) ** -0.5),
        'ssm_b_im': normal((DEPTH, SSM_GROUPS, SSM_STATE, SSM_GROUP), (2 * SSM_GROUP) ** -0.5),
        'ssm_c_re': normal((DEPTH, SSM_GROUPS, SSM_GROUP, SSM_STATE), (2 * SSM_STATE) ** -0.5),
        'ssm_c_im': normal((DEPTH, SSM_GROUPS, SSM_GROUP, SSM_STATE), (2 * SSM_STATE) ** -0.5),
        'ssm_d': normal((DEPTH, SSM_WIDTH), 1.0),
        'ssm_w_glu': normal((DEPTH, SSM_WIDTH, SSM_WIDTH), SSM_WIDTH ** -0.5),
        'ssm_b_glu': normal((DEPTH, SSM_WIDTH), 0.01),
        'mla_q_norm': gain((DEPTH, Q_LORA)),
        'mla_kv_norm': gain((DEPTH, KV_LORA)),
        'mla_w_uq': normal((DEPTH, Q_LORA, MLA_HEADS * (NOPE_DIM + ROPE_DIM)), Q_LORA ** -0.5),
        'mla_w_uk': normal((DEPTH, KV_LORA, MLA_HEADS, NOPE_DIM), KV_LORA ** -0.5),
        'mla_w_uv': normal((DEPTH, KV_LORA, MLA_HEADS, V_DIM), KV_LORA ** -0.5),
        'w_br_gla': normal((DEPTH, GLA_VW, D_MODEL), GLA_VW ** -0.5),
        'w_br_ssm': normal((DEPTH, SSM_WIDTH, D_MODEL), SSM_WIDTH ** -0.5),
        'w_br_mla': normal((DEPTH, MLA_VW, D_MODEL), MLA_VW ** -0.5),
        'w_out': normal((DEPTH, D_MODEL, D_MODEL), D_MODEL ** -0.5),
        'router_w': normal((D_MODEL, N_EXPERTS), D_MODEL ** -0.5),
        'router_bias': normal((N_EXPERTS,), 0.01),
        'moe_w_gu': normal((DEPTH, N_EXPERTS, D_MODEL, 2 * D_EXPERT), D_MODEL ** -0.5),
        'moe_w_down': normal((DEPTH, N_EXPERTS, D_EXPERT, D_MODEL), D_EXPERT ** -0.5),
    }


def reference(x_prompt, x_sample, cache_ckv, cache_krope, state_gla, state_ssm_re, state_ssm_im,
              page_table, c_prompt, c_sample, norm_mix, norm_ffn, norm_final, w_ada, b_ada, w_in,
              gla_w_a2, gla_b_a2, gla_norm, ssm_lambda_re, ssm_lambda_im, ssm_log_dt, ssm_b_re,
              ssm_b_im, ssm_c_re, ssm_c_im, ssm_d, ssm_w_glu, ssm_b_glu, mla_q_norm, mla_kv_norm,
              mla_w_uq, mla_w_uk, mla_w_uv, w_br_gla, w_br_ssm, w_br_mla, w_out, router_w,
              router_bias, moe_w_gu, moe_w_down):
    B, T = x_prompt.shape[:2]
    Td = x_sample.shape[1]
    past_len = page_table.shape[1] * PAGE_SIZE
    pos_p = jnp.arange(T, dtype=jnp.int32)
    pos_s = past_len + jnp.arange(Td, dtype=jnp.int32)
    gla0_p = jnp.zeros((B, GLA_HEADS, GLA_DK, GLA_DV), F32)
    ssm0_p = jnp.zeros((B, SSM_GROUPS, SSM_STATE), F32)
    xp, xs = x_prompt, x_sample
    st_p, st_s = [], []
    for l in range(DEPTH):
        lw = {
            'norm_mix': norm_mix[l], 'norm_ffn': norm_ffn[l], 'w_ada': w_ada[l], 'b_ada': b_ada[l],
            'w_in': w_in[l], 'gla_w_a2': gla_w_a2[l], 'gla_b_a2': gla_b_a2[l], 'gla_norm': gla_norm[l],
            'ssm_lambda_re': ssm_lambda_re[l], 'ssm_lambda_im': ssm_lambda_im[l],
            'ssm_log_dt': ssm_log_dt[l], 'ssm_b_re': ssm_b_re[l], 'ssm_b_im': ssm_b_im[l],
            'ssm_c_re': ssm_c_re[l], 'ssm_c_im': ssm_c_im[l], 'ssm_d': ssm_d[l],
            'ssm_w_glu': ssm_w_glu[l], 'ssm_b_glu': ssm_b_glu[l],
            'mla_q_norm': mla_q_norm[l], 'mla_kv_norm': mla_kv_norm[l], 'mla_w_uq': mla_w_uq[l],
            'mla_w_uk': mla_w_uk[l], 'mla_w_uv': mla_w_uv[l],
            'w_br_gla': w_br_gla[l], 'w_br_ssm': w_br_ssm[l], 'w_br_mla': w_br_mla[l],
            'w_out': w_out[l], 'moe_w_gu': moe_w_gu[l], 'moe_w_down': moe_w_down[l],
        }
        xp, sp = trunk_layer(xp, c_prompt, pos_p, lw, router_w, router_bias,
                             gla0_p, ssm0_p, ssm0_p, mla_attend_prompt)
        attend_s = functools.partial(mla_attend_sample, cache_ckv=cache_ckv, cache_krope=cache_krope,
                                     page_table=page_table, layer=l)
        xs, ss = trunk_layer(xs, c_sample, pos_s, lw, router_w, router_bias,
                             state_gla[l], state_ssm_re[l], state_ssm_im[l], attend_s)
        st_p.append(sp)
        st_s.append(ss)
    y_prompt = rmsnorm(xp, norm_final)
    y_sample = rmsnorm(xs, norm_final)
    new_ckv_prompt = jnp.stack([s[0] for s in st_p])
    new_krope_prompt = jnp.stack([s[1] for s in st_p])
    new_ckv_sample = jnp.stack([s[0] for s in st_s])
    new_krope_sample = jnp.stack([s[1] for s in st_s])
    new_gla_prompt = jnp.stack([s[2] for s in st_p])
    new_gla_sample = jnp.stack([s[2] for s in st_s])
    new_ssm_re_prompt = jnp.stack([s[3] for s in st_p])
    new_ssm_im_prompt = jnp.stack([s[4] for s in st_p])
    new_ssm_re_sample = jnp.stack([s[3] for s in st_s])
    new_ssm_im_sample = jnp.stack([s[4] for s in st_s])
    return (y_prompt, y_sample, new_ckv_prompt, new_krope_prompt, new_ckv_sample, new_krope_sample,
            new_gla_prompt, new_gla_sample, new_ssm_re_prompt, new_ssm_im_prompt,
            new_ssm_re_sample, new_ssm_im_sample)
```

```python
import functools
import math

import jax
import jax.numpy as jnp
import numpy as np
from jax import lax
from jax.experimental import pallas as pl
from jax.experimental.pallas import tpu as pltpu

F32 = jnp.float32
BF16 = jnp.bfloat16

GLA_TAU = 16.0
ROPE_THETA = 10000.0
RMS_EPS = 1e-6
N_EXPERT_GROUPS = 4
TOP_K = 2
N_MOD = 6
N_BRANCH = 3
SSM_SLAB_GROUPS = 8

VMEM_LIMIT_BYTES = 52 * 1024 * 1024
LANES = 128
SUBLANES = 8

NT_DIMS = (((1,), (1,)), ((), ()))


def _params(*sem):
    return pltpu.CompilerParams(dimension_semantics=sem, vmem_limit_bytes=VMEM_LIMIT_BYTES)


def _pick(n, pref):
    if n <= pref:
        return n
    t = pref
    while n % t:
        t //= 2
    return t


def _silu(x):
    return x * jax.nn.sigmoid(x)


def _mm_body(*refs, pre, epilogue, has_bias, res_mode):
    it = iter(refs)
    x_ref, w_ref = next(it), next(it)
    b_ref = next(it) if has_bias else None
    r_ref = g_ref = None
    if res_mode:
        r_ref, g_ref = next(it), next(it)
    o_ref, wb_ref = next(it), next(it)

    @pl.when(pl.program_id(1) == 0)
    def _():
        wb_ref[...] = w_ref[...].astype(BF16)

    x = x_ref[...]
    if pre == "silu":
        x = _silu(x.astype(F32))
    elif pre == "swiglu":
        f = x.shape[1] // 2
        xf = x.astype(F32)
        x = _silu(xf[:, :f]) * xf[:, f:]
    acc = jnp.dot(x.astype(BF16), wb_ref[...], preferred_element_type=F32)
    if has_bias:
        acc = acc + b_ref[...]
    if epilogue == "sigmoid":
        acc = jax.nn.sigmoid(acc)
    if res_mode == "group":
        acc = r_ref[...] + g_ref[0] * acc
    elif res_mode == "token":
        acc = r_ref[...] + g_ref[...] * acc
    o_ref[...] = acc.astype(o_ref.dtype)


def matmul(x, w, *, bias=None, pre=None, epilogue=None, out_dtype=F32, bm=512, bn=1024,
           res=None, gate=None, gate_chunk=0, rows_per_group=None, name="matmul"):
    M, K = x.shape
    N = w.shape[1]
    bm, bn = _pick(rows_per_group or M, bm), _pick(N, bn)
    nj = N // bn
    kw = K // 2 if pre == "swiglu" else K
    assert w.shape[0] == kw
    in_specs = [pl.BlockSpec((bm, K), lambda j, i: (i, 0)),
                pl.BlockSpec((kw, bn), lambda j, i: (0, j))]
    args = [x, w]
    if bias is not None:
        in_specs.append(pl.BlockSpec((1, bn), lambda j, i: (0, j)))
        args.append(bias.reshape(1, N))
    res_mode = None
    if res is not None:
        in_specs.append(pl.BlockSpec((bm, bn), lambda j, i: (i, j)))
        args.append(res)
        if rows_per_group is not None:
            res_mode = "group"
            assert rows_per_group % bm == 0
            rpg = rows_per_group // bm
            in_specs.append(pl.BlockSpec((1, 1, bn), lambda j, i: (i // rpg, 0, gate_chunk * nj + j)))
        else:
            res_mode = "token"
            in_specs.append(pl.BlockSpec((bm, bn), lambda j, i: (i, gate_chunk * nj + j)))
        args.append(gate)
    return pl.pallas_call(
        functools.partial(_mm_body, pre=pre, epilogue=epilogue, has_bias=bias is not None,
                          res_mode=res_mode),
        grid=(nj, M // bm),
        in_specs=in_specs,
        out_specs=pl.BlockSpec((bm, bn), lambda j, i: (i, j)),
        out_shape=jax.ShapeDtypeStruct((M, N), out_dtype),
        scratch_shapes=[pltpu.VMEM((kw, bn), BF16)],
        compiler_params=_params("arbitrary", "arbitrary"),
        name=name,
    )(*args)


def _norm_body(*refs, mode):
    if mode:
        x_ref, g_ref, sc_ref, sh_ref, o_ref = refs
    else:
        x_ref, g_ref, o_ref = refs
    x = x_ref[...].astype(F32)
    y = x * lax.rsqrt(jnp.mean(x * x, axis=-1, keepdims=True) + RMS_EPS) * g_ref[...]
    if mode == "group":
        y = y * (1.0 + sc_ref[0]) + sh_ref[0]
    elif mode == "token":
        y = y * (1.0 + sc_ref[...]) + sh_ref[...]
    o_ref[...] = y.astype(o_ref.dtype)


def norm_mod(x, g, mod=None, *, sc_chunk=0, sh_chunk=0, rows_per_group=None, out_dtype=BF16,
             bm=512, name="norm_mod"):
    M, D = x.shape
    bm = _pick(rows_per_group or M, bm)
    in_specs = [pl.BlockSpec((bm, D), lambda i: (i, 0)), pl.BlockSpec((1, D), lambda i: (0, 0))]
    args = [x, g.reshape(1, D)]
    mode = None
    if mod is not None:
        if rows_per_group is not None:
            mode = "group"
            rpg = rows_per_group // bm
            in_specs += [pl.BlockSpec((1, 1, D), lambda i: (i // rpg, 0, sc_chunk)),
                         pl.BlockSpec((1, 1, D), lambda i: (i // rpg, 0, sh_chunk))]
        else:
            mode = "token"
            in_specs += [pl.BlockSpec((bm, D), lambda i: (i, sc_chunk)),
                         pl.BlockSpec((bm, D), lambda i: (i, sh_chunk))]
        args += [mod, mod]
    return pl.pallas_call(
        functools.partial(_norm_body, mode=mode),
        grid=(M // bm,),
        in_specs=in_specs,
        out_specs=pl.BlockSpec((bm, D), lambda i: (i, 0)),
        out_shape=jax.ShapeDtypeStruct((M, D), out_dtype),
        compiler_params=_params("arbitrary"),
        name=name,
    )(*args)


def _log_sigmoid(z):
    return jnp.minimum(z, 0.0) - jnp.log1p(jnp.exp(-jnp.abs(z)))


def _gla_body(*refs, H, DK, DV, C, t_valid, has_s0):
    if has_s0:
        q_ref, k_ref, v_ref, g_ref, a_ref, wa_ref, ba_ref, ng_ref, s0_ref, o_ref, st_ref, sT_ref = refs
    else:
        q_ref, k_ref, v_ref, g_ref, a_ref, wa_ref, ba_ref, ng_ref, o_ref, st_ref, sT_ref = refs
    c = pl.program_id(1)

    @pl.when(c == 0)
    def _():
        for h in range(H):
            if has_s0:
                sT_ref[h] = s0_ref[0, h].T
            else:
                sT_ref[h] = jnp.zeros((DV, DK), F32)

    row = lax.broadcasted_iota(jnp.int32, (C, 1), 0)
    tri = row >= lax.broadcasted_iota(jnp.int32, (1, C), 1)
    trib = jnp.where(tri, 1.0, 0.0).astype(BF16)
    live = row < t_valid
    alr = a_ref[0].astype(BF16)
    mid = C // 2 - 1
    for h in range(H):
        ks = slice(h * DK, (h + 1) * DK)
        vs = slice(h * DV, (h + 1) * DV)
        z = jnp.dot(alr, wa_ref[:, ks], preferred_element_type=F32) + ba_ref[:, ks]
        la = _log_sigmoid(z) * (1.0 / GLA_TAU)
        if t_valid < C:
            la = jnp.where(live, la, 0.0)
        hi = la.astype(BF16)
        lo = (la - hi.astype(F32)).astype(BF16)
        b = (jnp.dot(trib, hi, preferred_element_type=F32)
             + jnp.dot(trib, lo, preferred_element_type=F32))
        b_mid = b[mid:mid + 1, :]
        b_last = b[C - 1:C, :]
        q = q_ref[0, :, ks].astype(F32) * (DK ** -0.5)
        k = k_ref[0, :, ks].astype(F32)
        if t_valid < C:
            k = jnp.where(live, k, 0.0)
        v = v_ref[0, :, vs].astype(BF16)
        qe = (q * jnp.exp(b - b_mid)).astype(BF16)
        ke = (k * jnp.exp(b_mid - b)).astype(BF16)
        a = lax.dot_general(qe, ke, NT_DIMS, preferred_element_type=F32)
        a = jnp.where(tri, a, 0.0)
        qb = (q * jnp.exp(b)).astype(BF16)
        kl = (k * jnp.exp(b_last - b)).astype(BF16)
        sT = sT_ref[h]
        o = (jnp.dot(a.astype(BF16), v, preferred_element_type=F32)
             + lax.dot_general(qb, sT.astype(BF16), NT_DIMS, preferred_element_type=F32))
        vT = v_ref[0, :, vs].astype(F32).T.astype(BF16)
        sT_ref[h] = jnp.exp(b_last) * sT + jnp.dot(vT, kl, preferred_element_type=F32)
        on = o * lax.rsqrt(jnp.mean(o * o, axis=-1, keepdims=True) + RMS_EPS) * ng_ref[...]
        gg = g_ref[0, :, vs].astype(F32)
        o_ref[0, :, vs] = (on * _silu(gg)).astype(o_ref.dtype)

    @pl.when(c == pl.num_programs(1) - 1)
    def _():
        for h in range(H):
            st_ref[0, h] = sT_ref[h].T


def gla_mixer(qkvg, misc, wa, ba, ng, s0, *, B, T, H, DK, DV, C, t_valid, alr_block):
    QK, VW = H * DK, H * DV
    nc = T // C
    in_specs = [
        pl.BlockSpec((1, C, QK), lambda b, c: (b, c, 0)),
        pl.BlockSpec((1, C, QK), lambda b, c: (b, c, 1)),
        pl.BlockSpec((1, C, VW), lambda b, c: (b, c, (2 * QK) // VW)),
        pl.BlockSpec((1, C, VW), lambda b, c: (b, c, (2 * QK) // VW + 1)),
        pl.BlockSpec((1, C, LANES), lambda b, c: (b, c, alr_block)),
        pl.BlockSpec((LANES, QK), lambda b, c: (0, 0)),
        pl.BlockSpec((1, QK), lambda b, c: (0, 0)),
        pl.BlockSpec((1, DV), lambda b, c: (0, 0)),
    ]
    args = [qkvg, qkvg, qkvg, qkvg, misc, wa, ba, ng]
    if s0 is not None:
        in_specs.append(pl.BlockSpec((1, H, DK, DV), lambda b, c: (b, 0, 0, 0)))
        args.append(s0)
    assert (2 * QK) % VW == 0
    return pl.pallas_call(
        functools.partial(_gla_body, H=H, DK=DK, DV=DV, C=C, t_valid=t_valid, has_s0=s0 is not None),
        grid=(B, nc),
        in_specs=in_specs,
        out_specs=[pl.BlockSpec((1, C, VW), lambda b, c: (b, c, 0)),
                   pl.BlockSpec((1, H, DK, DV), lambda b, c: (b, 0, 0, 0))],
        out_shape=[jax.ShapeDtypeStruct((B, T, VW), BF16),
                   jax.ShapeDtypeStruct((B, H, DK, DV), F32)],
        scratch_shapes=[pltpu.VMEM((H, DV, DK), F32)],
        compiler_params=_params("arbitrary", "arbitrary"),
        name="gla_mixer",
    )(*args)


def _gelu_tanh(y):
    return 0.5 * y * (1.0 + jnp.tanh(math.sqrt(2.0 / math.pi) * (y + 0.044715 * (y * y * y))))


def _s5_body(*refs, R, Tc, NQ, PQ, LC, has_x0):
    if has_x0:
        (u_ref, wb_ref, wc_ref, ab_ref, d_ref, wg_ref, bg_ref, x0r_ref, x0i_ref,
         o_ref, sr_ref, si_ref, xr, xi, zs, st_r, st_i) = refs
    else:
        (u_ref, wb_ref, wc_ref, ab_ref, d_ref, wg_ref, bg_ref,
         o_ref, sr_ref, si_ref, xr, xi, zs, st_r, st_i) = refs
    step = pl.program_id(0)
    S = NQ * PQ

    @pl.when(step == 0)
    def _():
        if has_x0:
            st_r[...] = x0r_ref[...]
            st_i[...] = x0i_ref[...]
        else:
            st_r[...] = jnp.zeros(st_r.shape, F32)
            st_i[...] = jnp.zeros(st_i.shape, F32)

    for q in range(NQ):
        uq = u_ref[:, q * LANES:(q + 1) * LANES].astype(BF16)
        r = jnp.dot(uq, wb_ref[q], preferred_element_type=F32)
        xr[:, q * PQ:(q + 1) * PQ] = r[:, :PQ]
        xi[:, q * PQ:(q + 1) * PQ] = r[:, PQ:]

    if R % SUBLANES == 0:
        for lc in range(S // LC):
            sl = slice(lc * LC, (lc + 1) * LC)
            ar = ab_ref[0, 0:1, sl]
            ai = ab_ref[1, 0:1, sl]
            cr, ci = st_r[:, sl], st_i[:, sl]
            for t in range(Tc):
                rs = slice(t * R, (t + 1) * R)
                nr = ar * cr - ai * ci + xr[rs, sl]
                ni = ar * ci + ai * cr + xi[rs, sl]
                xr[rs, sl] = nr
                xi[rs, sl] = ni
                cr, ci = nr, ni
            st_r[:, sl] = cr
            st_i[:, sl] = ci
    else:
        assert R * 2 == SUBLANES and Tc % 2 == 0
        lo_half = lax.broadcasted_iota(jnp.int32, (SUBLANES, LC), 0) < R
        for lc in range(S // LC):
            sl = slice(lc * LC, (lc + 1) * LC)
            ar = ab_ref[0, :, sl]
            ai = ab_ref[1, :, sl]

            def pair(j, carry, sl=sl, ar=ar, ai=ai):
                yr, yi = carry
                off = pl.multiple_of(j * SUBLANES, SUBLANES)
                br = xr[pl.ds(off, SUBLANES), sl]
                bi = xi[pl.ds(off, SUBLANES), sl]
                pr = pltpu.roll(yr, R, 0)
                pi_ = pltpu.roll(yi, R, 0)
                zr = ar * pr - ai * pi_ + br
                zi = ar * pi_ + ai * pr + bi
                qr = pltpu.roll(zr, R, 0)
                qi = pltpu.roll(zi, R, 0)
                wr = ar * qr - ai * qi + br
                wi = ar * qi + ai * qr + bi
                nr = jnp.where(lo_half, zr, wr)
                ni = jnp.where(lo_half, zi, wi)
                xr[pl.ds(off, SUBLANES), sl] = nr
                xi[pl.ds(off, SUBLANES), sl] = ni
                return nr, ni

            cr, ci = lax.fori_loop(0, Tc // 2, pair, (st_r[:, sl], st_i[:, sl]))
            st_r[:, sl] = cr
            st_i[:, sl] = ci

    for q in range(NQ):
        ps = slice(q * PQ, (q + 1) * PQ)
        ls = slice(q * LANES, (q + 1) * LANES)
        y = (jnp.dot(xr[:, ps].astype(BF16), wc_ref[0, q], preferred_element_type=F32)
             + jnp.dot(xi[:, ps].astype(BF16), wc_ref[1, q], preferred_element_type=F32))
        y = y + d_ref[:, ls] * u_ref[:, ls].astype(F32)
        zs[:, ls] = _gelu_tanh(y)
    z = zs[...]
    gate = jax.nn.sigmoid(jnp.dot(z.astype(BF16), wg_ref[...], preferred_element_type=F32) + bg_ref[...])
    o_ref[...] = (z * gate).astype(o_ref.dtype)

    @pl.when(step == pl.num_programs(0) - 1)
    def _():
        sr_ref[...] = st_r[...]
        si_ref[...] = st_i[...]


def s5_mixer(u_tm, u_block, wb, wc, ab, d, wg, bg, x0r, x0i, *, R, T, Tc, W):
    NQ, _, PQ2 = wb.shape
    PQ = PQ2 // 2
    S = NQ * PQ
    rows = Tc * R
    Rp = max(R, SUBLANES)
    LC = min(S, 512)
    in_specs = [
        pl.BlockSpec((rows, W), lambda s: (s, u_block)),
        pl.BlockSpec(wb.shape, lambda s: (0, 0, 0)),
        pl.BlockSpec(wc.shape, lambda s: (0, 0, 0, 0)),
        pl.BlockSpec(ab.shape, lambda s: (0, 0, 0)),
        pl.BlockSpec((1, W), lambda s: (0, 0)),
        pl.BlockSpec((W, W), lambda s: (0, 0)),
        pl.BlockSpec((1, W), lambda s: (0, 0)),
    ]
    args = [u_tm, wb, wc, ab, d, wg, bg]
    if x0r is not None:
        in_specs += [pl.BlockSpec((Rp, S), lambda s: (0, 0))] * 2
        args += [x0r, x0i]
    return pl.pallas_call(
        functools.partial(_s5_body, R=R, Tc=Tc, NQ=NQ, PQ=PQ, LC=LC, has_x0=x0r is not None),
        grid=(T // Tc,),
        in_specs=in_specs,
        out_specs=[pl.BlockSpec((rows, W), lambda s: (s, 0)),
                   pl.BlockSpec((Rp, S), lambda s: (0, 0)),
                   pl.BlockSpec((Rp, S), lambda s: (0, 0))],
        out_shape=[jax.ShapeDtypeStruct((T * R, W), BF16),
                   jax.ShapeDtypeStruct((Rp, S), F32),
                   jax.ShapeDtypeStruct((Rp, S), F32)],
        scratch_shapes=[pltpu.VMEM((rows, S), F32), pltpu.VMEM((rows, S), F32),
                        pltpu.VMEM((rows, W), F32),
                        pltpu.VMEM((Rp, S), F32), pltpu.VMEM((Rp, S), F32)],
        compiler_params=_params("arbitrary"),
        name="s5_mixer",
    )(*args)


def s5_weights(lam_re, lam_im, log_dt, b_re, b_im, c_re, c_im):
    G, P = lam_re.shape
    SG = b_re.shape[-1]
    NQ = G // SSM_SLAB_GROUPS
    dt = jnp.exp(log_dt.astype(F32))[:, None]
    lr, li = lam_re.astype(F32), lam_im.astype(F32)
    mag, ang = jnp.exp(lr * dt), li * dt
    ab_re, ab_im = mag * jnp.cos(ang), mag * jnp.sin(ang)
    den = lr * lr + li * li
    f_re = ((ab_re - 1.0) * lr + ab_im * li) / den
    f_im = (ab_im * lr - (ab_re - 1.0) * li) / den
    br, bi = b_re.astype(F32), b_im.astype(F32)
    bb_re = f_re[..., None] * br - f_im[..., None] * bi
    bb_im = f_re[..., None] * bi + f_im[..., None] * br
    eye = jnp.eye(SSM_SLAB_GROUPS, dtype=F32)

    def slab_in(bb):
        x = bb.reshape(NQ, SSM_SLAB_GROUPS, P, SG)
        return jnp.einsum("qgpi,gh->qgihp", x, eye).reshape(NQ, SSM_SLAB_GROUPS * SG, SSM_SLAB_GROUPS * P)

    def slab_out(cc):
        x = cc.reshape(NQ, SSM_SLAB_GROUPS, SG, P)
        return jnp.einsum("qgop,gh->qgpho", x, eye).reshape(NQ, SSM_SLAB_GROUPS * P, SSM_SLAB_GROUPS * SG)

    wb = jnp.concatenate([slab_in(bb_re), slab_in(bb_im)], axis=-1).astype(BF16)
    wc = jnp.stack([slab_out(c_re.astype(F32)), -slab_out(c_im.astype(F32))]).astype(BF16)
    ab = jnp.stack([jnp.broadcast_to(ab_re.reshape(1, G * P), (SUBLANES, G * P)),
                    jnp.broadcast_to(ab_im.reshape(1, G * P), (SUBLANES, G * P))])
    return wb, wc, ab


def _rope_128(x, cos, sin, half):
    first = lax.broadcasted_iota(jnp.int32, x.shape, 1) < half
    swapped = jnp.where(first, pltpu.roll(x, LANES - half, 1), pltpu.roll(x, half, 1))
    return x * cos + swapped * sin


def _mla_prep_body(cq_ref, ckv_ref, kr_ref, qg_ref, kvg_ref, wq_ref, wuk_ref, cos_ref, sin_ref,
                   q_ref, kcat_ref, ckvo_ref, kro_ref, *, H, NOPE, KV, ROPE, scale):
    cq = cq_ref[...].astype(F32)
    cqn = cq * lax.rsqrt(jnp.mean(cq * cq, axis=-1, keepdims=True) + RMS_EPS) * qg_ref[...]
    qall = jnp.dot(cqn.astype(BF16), wq_ref[...], preferred_element_type=F32)
    cos, sin = cos_ref[...], sin_ref[...]
    for h in range(H):
        qn = qall[:, h * NOPE:(h + 1) * NOPE]
        qr = qall[:, H * NOPE + h * LANES:H * NOPE + (h + 1) * LANES]
        q_lat = jnp.dot(qn.astype(BF16), wuk_ref[h], preferred_element_type=F32)
        q_ref[h, :, 0:KV] = (q_lat * scale).astype(q_ref.dtype)
        q_ref[h, :, KV:KV + LANES] = (_rope_128(qr, cos, sin, ROPE // 2) * scale).astype(q_ref.dtype)
    ckv = ckv_ref[...].astype(F32)
    ckvn = ckv * lax.rsqrt(jnp.mean(ckv * ckv, axis=-1, keepdims=True) + RMS_EPS) * kvg_ref[...]
    krot = _rope_128(kr_ref[...].astype(F32), cos, sin, ROPE // 2)
    ckvo_ref[...] = ckvn
    kro_ref[...] = krot[:, :ROPE]
    kcat_ref[:, 0:KV] = ckvn.astype(kcat_ref.dtype)
    kcat_ref[:, KV:KV + LANES] = krot.astype(kcat_ref.dtype)


def mla_prep(misc, qg, kvg, wq, wuk, cos, sin, *, H, QL, KV, NOPE, ROPE, bm=256):
    M = misc.shape[0]
    bm = _pick(M, bm)
    P = cos.shape[0]
    if P == M:
        tab_map = lambda i: (i, 0)
    else:
        assert P % bm == 0
        npb = P // bm
        tab_map = lambda i: (i % npb, 0)
    scale = (NOPE + ROPE) ** -0.5
    return pl.pallas_call(
        functools.partial(_mla_prep_body, H=H, NOPE=NOPE, KV=KV, ROPE=ROPE, scale=scale),
        grid=(M // bm,),
        in_specs=[
            pl.BlockSpec((bm, QL), lambda i: (i, 0)),
            pl.BlockSpec((bm, KV), lambda i: (i, QL // KV)),
            pl.BlockSpec((bm, LANES), lambda i: (i, (QL + KV) // LANES)),
            pl.BlockSpec((1, QL), lambda i: (0, 0)),
            pl.BlockSpec((1, KV), lambda i: (0, 0)),
            pl.BlockSpec(wq.shape, lambda i: (0, 0)),
            pl.BlockSpec(wuk.shape, lambda i: (0, 0, 0)),
            pl.BlockSpec((bm, LANES), tab_map),
            pl.BlockSpec((bm, LANES), tab_map),
        ],
        out_specs=[pl.BlockSpec((H, bm, KV + LANES), lambda i: (0, i, 0)),
                   pl.BlockSpec((bm, KV + LANES), lambda i: (i, 0)),
                   pl.BlockSpec((bm, KV), lambda i: (i, 0)),
                   pl.BlockSpec((bm, ROPE), lambda i: (i, 0))],
        out_shape=[jax.ShapeDtypeStruct((H, M, KV + LANES), BF16),
                   jax.ShapeDtypeStruct((M, KV + LANES), BF16),
                   jax.ShapeDtypeStruct((M, KV), F32),
                   jax.ShapeDtypeStruct((M, ROPE), F32)],
        compiler_params=_params("arbitrary"),
        name="mla_prep",
    )(misc, misc, misc, qg.reshape(1, QL), kvg.reshape(1, KV), wq, wuk, cos, sin)


def _attn_prompt_body(qi_tab, kj_tab, q_ref, k_ref, wuv_ref, o_ref, m_ref, l_ref, acc_ref,
                      *, H, bq, bk, KV, VD):
    p = pl.program_id(1)
    qi, kj = qi_tab[p], kj_tab[p]
    M = H * bq

    @pl.when(kj == 0)
    def _():
        m_ref[...] = jnp.full(m_ref.shape, -jnp.inf, F32)
        l_ref[...] = jnp.zeros(l_ref.shape, F32)
        acc_ref[...] = jnp.zeros(acc_ref.shape, F32)

    q = q_ref[...].reshape(M, q_ref.shape[-1])
    k = k_ref[...]
    s = lax.dot_general(q, k, NT_DIMS, preferred_element_type=F32)
    qpos = qi * bq + lax.broadcasted_iota(jnp.int32, (bq, bk), 0)
    kpos = kj * bk + lax.broadcasted_iota(jnp.int32, (bq, bk), 1)
    s = jnp.where((kpos <= qpos)[None], s.reshape(H, bq, bk), -jnp.inf).reshape(M, bk)
    m_old = m_ref[...]
    m_new = jnp.maximum(m_old, jnp.max(s, axis=-1, keepdims=True))
    alpha = jnp.exp(m_old - m_new)
    pr = jnp.exp(s - m_new)
    l_ref[...] = alpha * l_ref[...] + jnp.sum(pr, axis=-1, keepdims=True)
    acc_ref[...] = alpha * acc_ref[...] + jnp.dot(pr.astype(BF16), k[:, :KV], preferred_element_type=F32)
    m_ref[...] = m_new

    @pl.when(kj == ((qi + 1) * bq - 1) // bk)
    def _():
        o = acc_ref[...] / l_ref[...]
        for h in range(H):
            oh = o[h * bq:(h + 1) * bq].astype(BF16)
            o_ref[:, h * VD:(h + 1) * VD] = jnp.dot(oh, wuv_ref[h], preferred_element_type=F32).astype(o_ref.dtype)


def mla_attend_prompt(qcat, kcat, wuv, *, B, T, H, KV, VD, bq=128, bk=512):
    bq, bk = _pick(T, bq), _pick(T, bk)
    nq, nk = T // bq, T // bk
    pairs = [(qi, kj) for qi in range(nq) for kj in range(((qi + 1) * bq - 1) // bk + 1)]
    qi_tab = jnp.asarray(np.array([p[0] for p in pairs], np.int32))
    kj_tab = jnp.asarray(np.array([p[1] for p in pairs], np.int32))
    E = qcat.shape[-1]
    M = H * bq
    grid_spec = pltpu.PrefetchScalarGridSpec(
        num_scalar_prefetch=2,
        grid=(B, len(pairs)),
        in_specs=[
            pl.BlockSpec((H, bq, E), lambda b, p, qt, kt: (0, b * nq + qt[p], 0)),
            pl.BlockSpec((bk, E), lambda b, p, qt, kt: (b * nk + kt[p], 0)),
            pl.BlockSpec(wuv.shape, lambda b, p, qt, kt: (0, 0, 0)),
        ],
        out_specs=pl.BlockSpec((bq, H * VD), lambda b, p, qt, kt: (b * nq + qt[p], 0)),
        scratch_shapes=[pltpu.VMEM((M, 1), F32), pltpu.VMEM((M, 1), F32), pltpu.VMEM((M, KV), F32)],
    )
    return pl.pallas_call(
        functools.partial(_attn_prompt_body, H=H, bq=bq, bk=bk, KV=KV, VD=VD),
        grid_spec=grid_spec,
        out_shape=jax.ShapeDtypeStruct((B * T, H * VD), BF16),
        compiler_params=_params("arbitrary", "arbitrary"),
        name="mla_attend_prompt",
    )(qi_tab, kj_tab, qcat, kcat, wuv)


def _attn_sample_body(pt_ref, q_ref, kn_ref, *refs, H, Td, PPS, PAGE, KV, ROPE, VD):
    ck_refs = refs[:PPS]
    kr_refs = refs[PPS:2 * PPS]
    wuv_ref, o_ref, m_ref, l_ref, acc_ref = refs[2 * PPS:]
    j = pl.program_id(1)
    q = q_ref[0]
    M = Td * H
    ql, qr = q[:, :KV], q[:, KV:KV + ROPE]

    @pl.when(j == 0)
    def _():
        kn = kn_ref[0]
        TN = kn.shape[0]
        s = lax.dot_general(q, kn, NT_DIMS, preferred_element_type=F32)
        tok = lax.broadcasted_iota(jnp.int32, (M, TN), 0) // H
        col = lax.broadcasted_iota(jnp.int32, (M, TN), 1)
        s = jnp.where(col <= tok, s, -jnp.inf)
        m0 = jnp.max(s, axis=-1, keepdims=True)
        p0 = jnp.exp(s - m0)
        m_ref[...] = m0
        l_ref[...] = jnp.sum(p0, axis=-1, keepdims=True)
        acc_ref[...] = jnp.dot(p0.astype(BF16), kn[:, :KV], preferred_element_type=F32)

    cks = [ck_refs[i][...].astype(BF16) for i in range(PPS)]
    ss = []
    for i in range(PPS):
        kr = kr_refs[i][...].astype(BF16)
        ss.append(lax.dot_general(ql, cks[i], NT_DIMS, preferred_element_type=F32)
                  + lax.dot_general(qr, kr, NT_DIMS, preferred_element_type=F32))
    s = jnp.concatenate(ss, axis=1)
    m_old = m_ref[...]
    m_new = jnp.maximum(m_old, jnp.max(s, axis=-1, keepdims=True))
    alpha = jnp.exp(m_old - m_new)
    pr = jnp.exp(s - m_new).astype(BF16)
    l_ref[...] = alpha * l_ref[...] + jnp.sum(pr.astype(F32), axis=-1, keepdims=True)
    pv = jnp.dot(pr[:, 0:PAGE], cks[0], preferred_element_type=F32)
    for i in range(1, PPS):
        pv = pv + jnp.dot(pr[:, i * PAGE:(i + 1) * PAGE], cks[i], preferred_element_type=F32)
    acc_ref[...] = alpha * acc_ref[...] + pv
    m_ref[...] = m_new

    @pl.when(j == pl.num_programs(1) - 1)
    def _():
        o = (acc_ref[...] / l_ref[...]).astype(BF16)
        full = jnp.dot(o, wuv_ref[...], preferred_element_type=F32)
        rh = lax.broadcasted_iota(jnp.int32, (M, H * VD), 0) % H
        ch = lax.broadcasted_iota(jnp.int32, (M, H * VD), 1) // VD
        own = jnp.where(rh == ch, full, 0.0).reshape(Td, H, H * VD)
        o_ref[0] = jnp.sum(own, axis=1).astype(o_ref.dtype)


def mla_attend_sample(page_table, q_s, kn_s, cache_ckv, cache_krope, wuv_all, *, layer, H, Td, KV, ROPE, VD, pps=16):
    Bd, n_pages = page_table.shape
    PAGE = cache_ckv.shape[2]
    pps = _pick(n_pages, pps)
    E = q_s.shape[-1]
    TN = kn_s.shape[1]
    M = Td * H

    def page_map(i):
        return lambda s, j, pt: (layer, pt[s, j * pps + i], 0, 0)

    in_specs = [pl.BlockSpec((1, M, E), lambda s, j, pt: (s, 0, 0)),
                pl.BlockSpec((1, TN, E), lambda s, j, pt: (s, 0, 0))]
    in_specs += [pl.BlockSpec((None, None, PAGE, KV), page_map(i)) for i in range(pps)]
    in_specs += [pl.BlockSpec((None, None, PAGE, ROPE), page_map(i)) for i in range(pps)]
    in_specs += [pl.BlockSpec(wuv_all.shape, lambda s, j, pt: (0, 0))]
    grid_spec = pltpu.PrefetchScalarGridSpec(
        num_scalar_prefetch=1,
        grid=(Bd, n_pages // pps),
        in_specs=in_specs,
        out_specs=pl.BlockSpec((1, Td, H * VD), lambda s, j, pt: (s, 0, 0)),
        scratch_shapes=[pltpu.VMEM((M, 1), F32), pltpu.VMEM((M, 1), F32), pltpu.VMEM((M, KV), F32)],
    )
    return pl.pallas_call(
        functools.partial(_attn_sample_body, H=H, Td=Td, PPS=pps, PAGE=PAGE, KV=KV, ROPE=ROPE, VD=VD),
        grid_spec=grid_spec,
        out_shape=jax.ShapeDtypeStruct((Bd, Td, H * VD), BF16),
        compiler_params=_params("arbitrary", "arbitrary"),
        name="mla_attend_sample",
    )(page_table, q_s, kn_s, *([cache_ckv] * pps), *([cache_krope] * pps), wuv_all)


def _merge_body(x0, x1, x2, w0, w1, w2, g0, g1, g2, o_ref, wb0, wb1, wb2):
    @pl.when(pl.program_id(1) == 0)
    def _():
        wb0[...] = w0[...].astype(BF16)
        wb1[...] = w1[...].astype(BF16)
        wb2[...] = w2[...].astype(BF16)

    acc = g0[...].astype(F32) * jnp.dot(x0[...], wb0[...], preferred_element_type=F32)
    acc = acc + g1[...].astype(F32) * jnp.dot(x1[...], wb1[...], preferred_element_type=F32)
    acc = acc + g2[...].astype(F32) * jnp.dot(x2[...], wb2[...], preferred_element_type=F32)
    o_ref[...] = acc.astype(o_ref.dtype)


def merge_branches(xs, ws, gates, *, bm=512, bn=512):
    M = xs[0].shape[0]
    Ks = [x.shape[1] for x in xs]
    N = ws[0].shape[1]
    bm, bn = _pick(M, bm), _pick(N, bn)
    nj = N // bn
    in_specs = ([pl.BlockSpec((bm, k), lambda j, i: (i, 0)) for k in Ks]
                + [pl.BlockSpec((k, bn), lambda j, i: (0, j)) for k in Ks]
                + [pl.BlockSpec((bm, bn), (lambda c: (lambda j, i: (i, c * nj + j)))(c)) for c in range(3)])
    return pl.pallas_call(
        _merge_body,
        grid=(nj, M // bm),
        in_specs=in_specs,
        out_specs=pl.BlockSpec((bm, bn), lambda j, i: (i, j)),
        out_shape=jax.ShapeDtypeStruct((M, N), BF16),
        scratch_shapes=[pltpu.VMEM((k, bn), BF16) for k in Ks],
        compiler_params=_params("arbitrary", "arbitrary"),
        name="merge_branches",
    )(*xs, *ws, gates, gates, gates)


def _router_body(h_ref, rw_ref, rb_ref, idx_ref, w_ref, *, E, NG):
    logits = lax.dot_general(rw_ref[...], h_ref[...].astype(BF16), NT_DIMS, preferred_element_type=F32)
    sc = jax.nn.sigmoid(logits)
    sel = sc + rb_ref[...]
    per = E // NG
    s_rows = [sc[e:e + 1, :] for e in range(E)]
    x_rows = [sel[e:e + 1, :] for e in range(E)]
    gscore = []
    for g in range(NG):
        vals = x_rows[g * per:(g + 1) * per]
        top1 = vals[0]
        top2 = jnp.full_like(top1, -jnp.inf)
        for v in vals[1:]:
            top2 = jnp.maximum(top2, jnp.minimum(top1, v))
            top1 = jnp.maximum(top1, v)
        gscore.append(top1 + top2)
    best = jnp.zeros_like(gscore[0], dtype=jnp.int32)
    bestv = gscore[0]
    for g in range(1, NG):
        upd = gscore[g] > bestv
        best = jnp.where(upd, g, best)
        bestv = jnp.where(upd, gscore[g], bestv)
    masked = [jnp.where(best == (e // per), x_rows[e], -jnp.inf) for e in range(E)]

    def arg_top(vals):
        bv = vals[0]
        bi = jnp.zeros_like(best)
        bs = s_rows[0]
        for e in range(1, E):
            upd = vals[e] > bv
            bv = jnp.where(upd, vals[e], bv)
            bi = jnp.where(upd, e, bi)
            bs = jnp.where(upd, s_rows[e], bs)
        return bi, bs

    i1, s1 = arg_top(masked)
    i2, s2 = arg_top([jnp.where(i1 == e, -jnp.inf, masked[e]) for e in range(E)])
    tot = s1 + s2
    idx_ref[0:1, :] = i1
    idx_ref[1:2, :] = i2
    w_ref[0:1, :] = s1 / tot
    w_ref[1:2, :] = s2 / tot


def router(h, rwT, rb, *, bm=512):
    M, D = h.shape
    E = rwT.shape[0]
    bm = _pick(M, bm)
    return pl.pallas_call(
        functools.partial(_router_body, E=E, NG=N_EXPERT_GROUPS),
        grid=(M // bm,),
        in_specs=[pl.BlockSpec((bm, D), lambda i: (i, 0)),
                  pl.BlockSpec((E, D), lambda i: (0, 0)),
                  pl.BlockSpec((E, 1), lambda i: (0, 0))],
        out_specs=[pl.BlockSpec((TOP_K, bm), lambda i: (0, i)),
                   pl.BlockSpec((TOP_K, bm), lambda i: (0, i))],
        out_shape=[jax.ShapeDtypeStruct((TOP_K, M), jnp.int32),
                   jax.ShapeDtypeStruct((TOP_K, M), F32)],
        compiler_params=_params("arbitrary"),
        name="moe_router",
    )(h, rwT, rb)


def _moe_body(te_ref, nv_ref, tok_ref, dst_ref, h_hbm, wgu_ref, wd_ref, sw_ref, y_hbm,
              xbuf, ybuf, gsem, ssem, *, bm, F):
    i = pl.program_id(0)
    base = i * bm

    def gather_copy(r, src_row):
        return pltpu.make_async_copy(h_hbm.at[pl.ds(src_row, 1)], xbuf.at[pl.ds(r, 1)], gsem)

    def scatter_copy(r, dst_row):
        return pltpu.make_async_copy(ybuf.at[pl.ds(r, 1)], y_hbm.at[pl.ds(dst_row, 1)], ssem)

    @pl.when(nv_ref[i] > 0)
    def _():
        def g_start(r, c):
            gather_copy(r, tok_ref[base + r]).start()
            return c

        def g_wait(r, c):
            gather_copy(r, 0).wait()
            return c

        lax.fori_loop(0, bm, g_start, 0)
        lax.fori_loop(0, bm, g_wait, 0)
        x = xbuf[...].astype(BF16)
        ab = jnp.dot(x, wgu_ref[0], preferred_element_type=F32)
        t = (_silu(ab[:, :F]) * ab[:, F:]).astype(BF16)
        y = jnp.dot(t, wd_ref[0], preferred_element_type=F32)
        ybuf[...] = sw_ref[...] * y

        def s_start(r, c):
            d = dst_ref[base + r]

            @pl.when(d >= 0)
            def _():
                scatter_copy(r, d).start()
            return c

        def s_wait(r, c):
            d = dst_ref[base + r]

            @pl.when(d >= 0)
            def _():
                scatter_copy(r, d).wait()
            return c

        lax.fori_loop(0, bm, s_start, 0)
        lax.fori_loop(0, bm, s_wait, 0)


def moe_experts(h, wgu, wd, tile_expert, tile_nvalid, slot_token, slot_dst, slot_w, *, bm):
    N, D = h.shape
    E, _, F2 = wgu.shape
    F = F2 // 2
    n_slots = slot_token.shape[0]
    nt = n_slots // bm
    grid_spec = pltpu.PrefetchScalarGridSpec(
        num_scalar_prefetch=4,
        grid=(nt,),
        in_specs=[
            pl.BlockSpec(memory_space=pl.ANY),
            pl.BlockSpec((1, D, F2), lambda i, te, nv, tk, ds: (te[i], 0, 0)),
            pl.BlockSpec((1, F, D), lambda i, te, nv, tk, ds: (te[i], 0, 0)),
            pl.BlockSpec((bm, 1), lambda i, te, nv, tk, ds: (i, 0)),
        ],
        out_specs=pl.BlockSpec(memory_space=pl.ANY),
        scratch_shapes=[pltpu.VMEM((bm, D), F32), pltpu.VMEM((bm, D), F32),
                        pltpu.SemaphoreType.DMA(()), pltpu.SemaphoreType.DMA(())],
    )
    return pl.pallas_call(
        functools.partial(_moe_body, bm=bm, F=F),
        grid_spec=grid_spec,
        out_shape=jax.ShapeDtypeStruct((TOP_K * N, D), F32),
        compiler_params=_params("arbitrary"),
        name="moe_experts",
    )(tile_expert, tile_nvalid, slot_token, slot_dst, h, wgu, wd, slot_w)


def moe_plan(idx, w, *, E, bm):
    K, N = idx.shape
    n_pairs = K * N
    n_slots = ((n_pairs + E * (bm - 1)) // bm) * bm
    flat_e = idx.reshape(-1)
    onehot = (flat_e[:, None] == jnp.arange(E, dtype=jnp.int32)[None, :]).astype(jnp.int32)
    csum = jnp.cumsum(onehot, axis=0)
    pos = jnp.sum((csum - onehot) * onehot, axis=1)
    counts = csum[-1]
    padded = ((counts + bm - 1) // bm) * bm
    ends = jnp.cumsum(padded)
    offs = ends - padded
    slot = offs[flat_e] + pos
    pair = jnp.arange(n_pairs, dtype=jnp.int32)
    slot_token = jnp.zeros((n_slots,), jnp.int32).at[slot].set(pair % N)
    slot_dst = jnp.full((n_slots,), -1, jnp.int32).at[slot].set(pair)
    slot_w = jnp.zeros((n_slots,), F32).at[slot].set(w.reshape(-1)).reshape(n_slots, 1)
    tile_start = jnp.arange(n_slots // bm, dtype=jnp.int32) * bm
    tile_expert = jnp.minimum(jnp.searchsorted(ends, tile_start, side="right"), E - 1).astype(jnp.int32)
    tile_nvalid = jnp.clip(offs[tile_expert] + counts[tile_expert] - tile_start, 0, bm).astype(jnp.int32)
    return tile_expert, tile_nvalid, slot_token, slot_dst, slot_w


def _combine_body(*refs, mode, final):
    it = iter(refs)
    x_ref, y0_ref, y1_ref, g_ref = next(it), next(it), next(it), next(it)
    ng_ref = next(it) if final else None
    o_ref = next(it)
    g = g_ref[0] if mode == "group" else g_ref[...]
    x = x_ref[...] + g * (y0_ref[...] + y1_ref[...])
    if final:
        x = x * lax.rsqrt(jnp.mean(x * x, axis=-1, keepdims=True) + RMS_EPS) * ng_ref[...]
    o_ref[...] = x


def moe_combine(x, y2, mod, *, row0, n_all, gate_chunk, rows_per_group=None, final_g=None, bm=512):
    M, D = x.shape
    bm = _pick(rows_per_group or M, bm)
    assert row0 % bm == 0 and n_all % bm == 0
    b0, b1 = row0 // bm, (n_all + row0) // bm
    in_specs = [pl.BlockSpec((bm, D), lambda i: (i, 0)),
                pl.BlockSpec((bm, D), lambda i: (b0 + i, 0)),
                pl.BlockSpec((bm, D), lambda i: (b1 + i, 0))]
    if rows_per_group is not None:
        mode = "group"
        rpg = rows_per_group // bm
        in_specs.append(pl.BlockSpec((1, 1, D), lambda i: (i // rpg, 0, gate_chunk)))
    else:
        mode = "token"
        in_specs.append(pl.BlockSpec((bm, D), lambda i: (i, gate_chunk)))
    args = [x, y2, y2, mod]
    if final_g is not None:
        in_specs.append(pl.BlockSpec((1, D), lambda i: (0, 0)))
        args.append(final_g.reshape(1, D))
    return pl.pallas_call(
        functools.partial(_combine_body, mode=mode, final=final_g is not None),
        grid=(M // bm,),
        in_specs=in_specs,
        out_specs=pl.BlockSpec((bm, D), lambda i: (i, 0)),
        out_shape=jax.ShapeDtypeStruct((M, D), F32),
        compiler_params=_params("arbitrary"),
        name="moe_combine",
    )(*args)


def _rope_tables(pos, rope_dim):
    half = rope_dim // 2
    inv = ROPE_THETA ** (-jnp.arange(half, dtype=F32) / half)
    ang = pos.astype(F32)[:, None] * inv[None, :]
    cos, sin = jnp.cos(ang), jnp.sin(ang)
    pad = jnp.zeros((pos.shape[0], LANES - rope_dim), F32)
    return (jnp.concatenate([cos, cos, pad], axis=1), jnp.concatenate([-sin, sin, pad], axis=1))


def _pad_cols(w, n):
    return jnp.pad(w, ((0, 0), (0, n - w.shape[1])))


def kernel(x_prompt, x_sample, cache_ckv, cache_krope, state_gla, state_ssm_re, state_ssm_im,
           page_table, c_prompt, c_sample, norm_mix, norm_ffn, norm_final, w_ada, b_ada, w_in,
           gla_w_a2, gla_b_a2, gla_norm, ssm_lambda_re, ssm_lambda_im, ssm_log_dt, ssm_b_re,
           ssm_b_im, ssm_c_re, ssm_c_im, ssm_d, ssm_w_glu, ssm_b_glu, mla_q_norm, mla_kv_norm,
           mla_w_uq, mla_w_uk, mla_w_uv, w_br_gla, w_br_ssm, w_br_mla, w_out, router_w,
           router_bias, moe_w_gu, moe_w_down):
    B, T, D = x_prompt.shape
    Bd, Td = x_sample.shape[:2]
    depth = w_in.shape[0]
    _, _, H, DK, DV = state_gla.shape
    QK, VW = H * DK, H * DV
    RANK = gla_w_a2.shape[1]
    G, P = ssm_lambda_re.shape[1:]
    SW = ssm_d.shape[1]
    QL = mla_q_norm.shape[1]
    KV, MH, NOPE = mla_w_uk.shape[1:]
    ROPE = cache_krope.shape[-1]
    VD = mla_w_uv.shape[-1]
    E = router_w.shape[1]
    F = moe_w_down.shape[2]
    n_pages = page_table.shape[1]
    PAGE = cache_ckv.shape[2]
    past_len = n_pages * PAGE
    Np, Ns = B * T, Bd * Td
    n_all = Np + Ns
    Tdp = SUBLANES
    TN = 16

    o_q, o_k, o_v, o_g = 0, QK, 2 * QK, 2 * QK + VW
    o_a = o_g + VW
    o_u = o_a + RANK
    o_cq = o_u + SW
    o_ckv = o_cq + QL
    o_kr = o_ckv + KV
    o_gt = o_kr + ROPE

    cos_p, sin_p = _rope_tables(jnp.arange(T, dtype=jnp.int32), ROPE)
    cos_s, sin_s = _rope_tables(past_len + jnp.arange(Td, dtype=jnp.int32), ROPE)
    cos_s, sin_s = jnp.tile(cos_s, (Bd, 1)), jnp.tile(sin_s, (Bd, 1))

    c_all = jnp.concatenate([c_prompt, c_sample], axis=0)
    rwT = router_w.T.astype(BF16)
    rb = router_bias.astype(F32).reshape(E, 1)
    moe_bm = 256

    xp = x_prompt.reshape(Np, D)
    xs = x_sample.reshape(Ns, D)
    outs = {k: [] for k in ("ckv_p", "kr_p", "ckv_s", "kr_s", "gla_p", "gla_s",
                            "sre_p", "sim_p", "sre_s", "sim_s")}
    y_p = y_s = None
    for l in range(depth):
        wl = w_in[l]
        w_qkvg = wl[:, :o_a]
        w_misc = jnp.concatenate([wl[:, o_cq:o_kr], _pad_cols(wl[:, o_kr:o_gt], LANES),
                                  _pad_cols(wl[:, o_a:o_u], LANES)], axis=1)
        misc_w = w_misc.shape[1]
        w_u = wl[:, o_u:o_cq]
        w_gt = wl[:, o_gt:]
        alr_block = (QL + KV + LANES) // LANES
        wa = jnp.pad(gla_w_a2[l], ((0, LANES - RANK), (0, 0))).astype(BF16)
        ba = gla_b_a2[l].astype(F32).reshape(1, QK)
        ng = gla_norm[l].astype(F32).reshape(1, DV)
        wb, wc, ab = s5_weights(ssm_lambda_re[l], ssm_lambda_im[l], ssm_log_dt[l], ssm_b_re[l],
                                ssm_b_im[l], ssm_c_re[l], ssm_c_im[l])
        sd = ssm_d[l].astype(F32).reshape(1, SW)
        wglu = ssm_w_glu[l].astype(BF16)
        bglu = ssm_b_glu[l].astype(F32).reshape(1, SW)
        wq3 = mla_w_uq[l].reshape(QL, MH, NOPE + ROPE)
        wq = jnp.concatenate([wq3[:, :, :NOPE].reshape(QL, MH * NOPE),
                              jnp.pad(wq3[:, :, NOPE:], ((0, 0), (0, 0), (0, LANES - ROPE))).reshape(QL, MH * LANES)],
                             axis=1).astype(BF16)
        wuk = jnp.transpose(mla_w_uk[l], (1, 2, 0)).astype(BF16)
        wuv = jnp.transpose(mla_w_uv[l], (1, 0, 2)).astype(BF16)
        wuv_all = mla_w_uv[l].reshape(KV, MH * VD).astype(BF16)
        wgu_b = moe_w_gu[l].astype(BF16)
        wd_b = moe_w_down[l].astype(BF16)

        mod = matmul(c_all, w_ada[l], bias=b_ada[l], pre="silu", name="ada_mod")
        mod_p = mod[:B].reshape(B, 1, N_MOD * D)
        mod_s = jnp.repeat(mod[B:], Td, axis=0)

        def mixers(x, *, prompt):
            nb, nt = (B, T) if prompt else (Bd, Td)
            n = nb * nt
            rpg = T if prompt else None
            m = mod_p if prompt else mod_s
            h = norm_mod(x, norm_mix[l], m, sc_chunk=1, sh_chunk=0, rows_per_group=rpg)
            qkvg = matmul(h, w_qkvg, out_dtype=BF16, name="in_qkvg")
            misc = matmul(h, w_misc, out_dtype=F32, name="in_misc")
            u = matmul(h, w_u, out_dtype=F32, name="in_u")
            gates = matmul(h, w_gt, epilogue="sigmoid", out_dtype=BF16, name="in_gates")
            if prompt:
                o_gla, gla_s = gla_mixer(qkvg.reshape(nb, nt, -1), misc.reshape(nb, nt, -1), wa, ba, ng, None,
                                         B=nb, T=nt, H=H, DK=DK, DV=DV, C=math.gcd(nt, 64), t_valid=math.gcd(nt, 64),
                                         alr_block=alr_block)
                o_gla = o_gla.reshape(n, VW)
            else:
                padt = ((0, 0), (0, Tdp - nt), (0, 0))
                o_gla, gla_s = gla_mixer(jnp.pad(qkvg.reshape(nb, nt, -1), padt),
                                         jnp.pad(misc.reshape(nb, nt, -1), padt), wa, ba, ng, state_gla[l],
                                         B=nb, T=Tdp, H=H, DK=DK, DV=DV, C=Tdp, t_valid=nt,
                                         alr_block=alr_block)
                o_gla = o_gla[:, :nt].reshape(n, VW)
            u_tm = u.reshape(nb, nt, SW).transpose(1, 0, 2).reshape(n, SW)
            if prompt:
                o_ssm, s_re, s_im = s5_mixer(u_tm, 0, wb, wc, ab, sd, wglu, bglu, None, None,
                                             R=nb, T=nt, Tc=_pick(nt, 64), W=SW)
                s_re, s_im = s_re[nb:2 * nb], s_im[nb:2 * nb]
            else:
                o_ssm, s_re, s_im = s5_mixer(u_tm, 0, wb, wc, ab, sd, wglu, bglu,
                                             state_ssm_re[l].reshape(nb, G * P), state_ssm_im[l].reshape(nb, G * P),
                                             R=nb, T=nt, Tc=_pick(nt, 2), W=SW)
            o_ssm = o_ssm.reshape(nt, nb, SW).transpose(1, 0, 2).reshape(n, SW)
            cos, sin = (cos_p, sin_p) if prompt else (cos_s, sin_s)
            qcat, kcat, ckv_n, kr_r = mla_prep(misc, mla_q_norm[l], mla_kv_norm[l], wq, wuk, cos, sin,
                                               H=MH, QL=QL, KV=KV, NOPE=NOPE, ROPE=ROPE)
            if prompt:
                o_mla = mla_attend_prompt(qcat, kcat, wuv, B=nb, T=nt, H=MH, KV=KV, VD=VD)
            else:
                q_s = qcat.reshape(MH, nb, nt, -1).transpose(1, 2, 0, 3).reshape(nb, nt * MH, -1)
                kn_s = jnp.pad(kcat.reshape(nb, nt, -1), ((0, 0), (0, TN - nt), (0, 0)))
                o_mla = mla_attend_sample(page_table, q_s, kn_s, cache_ckv, cache_krope, wuv_all, layer=l,
                                          H=MH, Td=nt, KV=KV, ROPE=ROPE, VD=VD).reshape(n, MH * VD)
            merged = merge_branches([o_gla, o_ssm, o_mla], [w_br_gla[l], w_br_ssm[l], w_br_mla[l]], gates)
            x = matmul(merged, w_out[l], res=x, gate=m, gate_chunk=2, rows_per_group=rpg, name="out_proj")
            hf = norm_mod(x, norm_ffn[l], m, sc_chunk=4, sh_chunk=3, rows_per_group=rpg, out_dtype=F32,
                          name="norm_ffn")
            state = (ckv_n.reshape(nb, nt, KV), kr_r.reshape(nb, nt, ROPE), gla_s,
                     s_re.reshape(nb, G, P), s_im.reshape(nb, G, P))
            return x, hf, state

        xp, hf_p, st_p = mixers(xp, prompt=True)
        xs, hf_s, st_s = mixers(xs, prompt=False)
        for k, v in zip(("ckv_p", "kr_p", "gla_p", "sre_p", "sim_p"), st_p):
            outs[k].append(v)
        for k, v in zip(("ckv_s", "kr_s", "gla_s", "sre_s", "sim_s"), st_s):
            outs[k].append(v)

        hf = jnp.concatenate([hf_p, hf_s], axis=0)
        idx, wts = router(hf, rwT, rb)
        plan = moe_plan(idx, wts, E=E, bm=moe_bm)
        y2 = moe_experts(hf, wgu_b, wd_b, *plan, bm=moe_bm)
        last = l == depth - 1
        fg = norm_final if last else None
        xp = moe_combine(xp, y2, mod_p, row0=0, n_all=n_all, gate_chunk=5, rows_per_group=T, final_g=fg)
        xs = moe_combine(xs, y2, mod_s, row0=Np, n_all=n_all, gate_chunk=5, final_g=fg)
        if last:
            y_p, y_s = xp, xs

    st = lambda k: jnp.stack(outs[k])
    return (y_p.reshape(B, T, D), y_s.reshape(Bd, Td, D),
            st("ckv_p"), st("kr_p"), st("ckv_s"), st("kr_s"), st("gla_p"), st("gla_s"),
            st("sre_p"), st("sim_p"), st("sre_s"), st("sim_s"))
```

```python
import functools
import math

import jax
import jax.numpy as jnp
import numpy as np
from jax import lax
from jax.experimental import pallas as pl
from jax.experimental.pallas import tpu as pltpu

F32 = jnp.float32
BF16 = jnp.bfloat16

GLA_TAU = 16.0
ROPE_THETA = 10000.0
RMS_EPS = 1e-6
N_EXPERT_GROUPS = 4
TOP_K = 2
N_MOD = 6
N_BRANCH = 3
SSM_SLAB_GROUPS = 8

VMEM_LIMIT_BYTES = 52 * 1024 * 1024
LANES = 128
SUBLANES = 8

NT_DIMS = (((1,), (1,)), ((), ()))


def _params(*sem):
    return pltpu.CompilerParams(dimension_semantics=sem, vmem_limit_bytes=VMEM_LIMIT_BYTES)


def _pick(n, pref):
    if n <= pref:
        return n
    t = pref
    while n % t:
        t //= 2
    return t


def _silu(x):
    return x * jax.nn.sigmoid(x)


def _mm_body(*refs, pre, epilogue, has_bias, res_mode):
    it = iter(refs)
    x_ref, w_ref = next(it), next(it)
    b_ref = next(it) if has_bias else None
    r_ref = g_ref = None
    if res_mode:
        r_ref, g_ref = next(it), next(it)
    o_ref, wb_ref = next(it), next(it)

    @pl.when(pl.program_id(1) == 0)
    def _():
        wb_ref[...] = w_ref[...].astype(BF16)

    x = x_ref[...]
    if pre == "silu":
        x = _silu(x.astype(F32))
    elif pre == "swiglu":
        f = x.shape[1] // 2
        xf = x.astype(F32)
        x = _silu(xf[:, :f]) * xf[:, f:]
    acc = jnp.dot(x.astype(BF16), wb_ref[...], preferred_element_type=F32)
    if has_bias:
        acc = acc + b_ref[...]
    if epilogue == "sigmoid":
        acc = jax.nn.sigmoid(acc)
    if res_mode == "group":
        acc = r_ref[...] + g_ref[0] * acc
    elif res_mode == "token":
        acc = r_ref[...] + g_ref[...] * acc
    o_ref[...] = acc.astype(o_ref.dtype)


def matmul(x, w, *, bias=None, pre=None, epilogue=None, out_dtype=F32, bm=512, bn=1024,
           res=None, gate=None, gate_chunk=0, rows_per_group=None, layer=None, name="matmul"):
    M, K = x.shape
    N = w.shape[-1]
    bm, bn = _pick(rows_per_group or M, bm), _pick(N, bn)
    nj = N // bn
    kw = K // 2 if pre == "swiglu" else K
    assert w.shape[-2] == kw
    if layer is None:
        w_spec = pl.BlockSpec((kw, bn), lambda j, i: (0, j))
    else:
        w_spec = pl.BlockSpec((None, kw, bn), lambda j, i: (layer, 0, j))
    in_specs = [pl.BlockSpec((bm, K), lambda j, i: (i, 0)), w_spec]
    args = [x, w]
    if bias is not None:
        in_specs.append(pl.BlockSpec((1, bn), lambda j, i: (0, j)))
        args.append(bias.reshape(1, N))
    res_mode = None
    if res is not None:
        in_specs.append(pl.BlockSpec((bm, bn), lambda j, i: (i, j)))
        args.append(res)
        if rows_per_group is not None:
            res_mode = "group"
            assert rows_per_group % bm == 0
            rpg = rows_per_group // bm
            in_specs.append(pl.BlockSpec((1, 1, bn), lambda j, i: (i // rpg, 0, gate_chunk * nj + j)))
        else:
            res_mode = "token"
            in_specs.append(pl.BlockSpec((bm, bn), lambda j, i: (i, gate_chunk * nj + j)))
        args.append(gate)
    return pl.pallas_call(
        functools.partial(_mm_body, pre=pre, epilogue=epilogue, has_bias=bias is not None,
                          res_mode=res_mode),
        grid=(nj, M // bm),
        in_specs=in_specs,
        out_specs=pl.BlockSpec((bm, bn), lambda j, i: (i, j)),
        out_shape=jax.ShapeDtypeStruct((M, N), out_dtype),
        scratch_shapes=[pltpu.VMEM((kw, bn), BF16)],
        compiler_params=_params("arbitrary", "arbitrary"),
        name=name,
    )(*args)


def _norm_body(*refs, mode):
    if mode:
        x_ref, g_ref, sc_ref, sh_ref, o_ref = refs
    else:
        x_ref, g_ref, o_ref = refs
    x = x_ref[...].astype(F32)
    y = x * lax.rsqrt(jnp.mean(x * x, axis=-1, keepdims=True) + RMS_EPS) * g_ref[...]
    if mode == "group":
        y = y * (1.0 + sc_ref[0]) + sh_ref[0]
    elif mode == "token":
        y = y * (1.0 + sc_ref[...]) + sh_ref[...]
    o_ref[...] = y.astype(o_ref.dtype)


def norm_mod(x, g, mod=None, *, sc_chunk=0, sh_chunk=0, rows_per_group=None, out_dtype=BF16,
             bm=512, name="norm_mod"):
    M, D = x.shape
    bm = _pick(rows_per_group or M, bm)
    in_specs = [pl.BlockSpec((bm, D), lambda i: (i, 0)), pl.BlockSpec((1, D), lambda i: (0, 0))]
    args = [x, g.reshape(1, D)]
    mode = None
    if mod is not None:
        if rows_per_group is not None:
            mode = "group"
            rpg = rows_per_group // bm
            in_specs += [pl.BlockSpec((1, 1, D), lambda i: (i // rpg, 0, sc_chunk)),
                         pl.BlockSpec((1, 1, D), lambda i: (i // rpg, 0, sh_chunk))]
        else:
            mode = "token"
            in_specs += [pl.BlockSpec((bm, D), lambda i: (i, sc_chunk)),
                         pl.BlockSpec((bm, D), lambda i: (i, sh_chunk))]
        args += [mod, mod]
    return pl.pallas_call(
        functools.partial(_norm_body, mode=mode),
        grid=(M // bm,),
        in_specs=in_specs,
        out_specs=pl.BlockSpec((bm, D), lambda i: (i, 0)),
        out_shape=jax.ShapeDtypeStruct((M, D), out_dtype),
        compiler_params=_params("arbitrary"),
        name=name,
    )(*args)


def _log_sigmoid(z):
    return jnp.minimum(z, 0.0) - jnp.log1p(jnp.exp(-jnp.abs(z)))


def _gla_body(*refs, H, DK, DV, C, t_valid, has_s0, Bb):
    if has_s0:
        q_ref, k_ref, v_ref, g_ref, a_ref, wa_ref, ba_ref, ng_ref, s0_ref, o_ref, st_ref, sT_ref = refs
    else:
        q_ref, k_ref, v_ref, g_ref, a_ref, wa_ref, ba_ref, ng_ref, o_ref, st_ref, sT_ref = refs
    c = pl.program_id(1)

    @pl.when(c == 0)
    def _():
        for bb in range(Bb):
            for h in range(H):
                if has_s0:
                    sT_ref[bb * H + h] = s0_ref[bb, h].T
                else:
                    sT_ref[bb * H + h] = jnp.zeros((DV, DK), F32)

    row = lax.broadcasted_iota(jnp.int32, (C, 1), 0)
    tri = row >= lax.broadcasted_iota(jnp.int32, (1, C), 1)
    trib = jnp.where(tri, 1.0, 0.0).astype(BF16)
    live = row < t_valid
    mid = C // 2 - 1
    for bb, h in [(bb, h) for bb in range(Bb) for h in range(H)]:
        alr = a_ref[bb].astype(BF16)
        ks = slice(h * DK, (h + 1) * DK)
        vs = slice(h * DV, (h + 1) * DV)
        z = jnp.dot(alr, wa_ref[:, ks], preferred_element_type=F32) + ba_ref[:, ks]
        la = _log_sigmoid(z) * (1.0 / GLA_TAU)
        if t_valid < C:
            la = jnp.where(live, la, 0.0)
        hi = la.astype(BF16)
        lo = (la - hi.astype(F32)).astype(BF16)
        b = (jnp.dot(trib, hi, preferred_element_type=F32)
             + jnp.dot(trib, lo, preferred_element_type=F32))
        b_mid = b[mid:mid + 1, :]
        b_last = b[C - 1:C, :]
        q = q_ref[bb, :, ks].astype(F32) * (DK ** -0.5)
        k = k_ref[bb, :, ks].astype(F32)
        if t_valid < C:
            k = jnp.where(live, k, 0.0)
        v = v_ref[bb, :, vs].astype(BF16)
        qe = (q * jnp.exp(b - b_mid)).astype(BF16)
        ke = (k * jnp.exp(b_mid - b)).astype(BF16)
        a = lax.dot_general(qe, ke, NT_DIMS, preferred_element_type=F32)
        a = jnp.where(tri, a, 0.0)
        qb = (q * jnp.exp(b)).astype(BF16)
        kl = (k * jnp.exp(b_last - b)).astype(BF16)
        sT = sT_ref[bb * H + h]
        o = (jnp.dot(a.astype(BF16), v, preferred_element_type=F32)
             + lax.dot_general(qb, sT.astype(BF16), NT_DIMS, preferred_element_type=F32))
        vT = v_ref[bb, :, vs].astype(F32).T.astype(BF16)
        sT_ref[bb * H + h] = jnp.exp(b_last) * sT + jnp.dot(vT, kl, preferred_element_type=F32)
        on = o * lax.rsqrt(jnp.mean(o * o, axis=-1, keepdims=True) + RMS_EPS) * ng_ref[...]
        gg = g_ref[bb, :, vs].astype(F32)
        o_ref[bb, :, vs] = (on * _silu(gg)).astype(o_ref.dtype)

    @pl.when(c == pl.num_programs(1) - 1)
    def _():
        for bb in range(Bb):
            for h in range(H):
                st_ref[bb, h] = sT_ref[bb * H + h].T


def gla_mixer(qkvg, misc, wa, ba, ng, s0, *, B, T, H, DK, DV, C, t_valid, alr_block, Bb=1):
    QK, VW = H * DK, H * DV
    nc = T // C
    assert B % Bb == 0
    in_specs = [
        pl.BlockSpec((Bb, C, QK), lambda b, c: (b, c, 0)),
        pl.BlockSpec((Bb, C, QK), lambda b, c: (b, c, 1)),
        pl.BlockSpec((Bb, C, VW), lambda b, c: (b, c, (2 * QK) // VW)),
        pl.BlockSpec((Bb, C, VW), lambda b, c: (b, c, (2 * QK) // VW + 1)),
        pl.BlockSpec((Bb, C, LANES), lambda b, c: (b, c, alr_block)),
        pl.BlockSpec((LANES, QK), lambda b, c: (0, 0)),
        pl.BlockSpec((1, QK), lambda b, c: (0, 0)),
        pl.BlockSpec((1, DV), lambda b, c: (0, 0)),
    ]
    args = [qkvg, qkvg, qkvg, qkvg, misc, wa, ba, ng]
    if s0 is not None:
        in_specs.append(pl.BlockSpec((Bb, H, DK, DV), lambda b, c: (b, 0, 0, 0)))
        args.append(s0)
    assert (2 * QK) % VW == 0
    return pl.pallas_call(
        functools.partial(_gla_body, H=H, DK=DK, DV=DV, C=C, t_valid=t_valid, has_s0=s0 is not None, Bb=Bb),
        grid=(B // Bb, nc),
        in_specs=in_specs,
        out_specs=[pl.BlockSpec((Bb, C, VW), lambda b, c: (b, c, 0)),
                   pl.BlockSpec((Bb, H, DK, DV), lambda b, c: (b, 0, 0, 0))],
        out_shape=[jax.ShapeDtypeStruct((B, T, VW), BF16),
                   jax.ShapeDtypeStruct((B, H, DK, DV), F32)],
        scratch_shapes=[pltpu.VMEM((Bb * H, DV, DK), F32)],
        compiler_params=_params("arbitrary", "arbitrary"),
        name="gla_mixer",
    )(*args)


def _gelu_tanh(y):
    return 0.5 * y * (1.0 + jnp.tanh(math.sqrt(2.0 / math.pi) * (y + 0.044715 * (y * y * y))))


def _s5_body(*refs, R, Tc, NQ, PQ, LC, has_x0):
    if has_x0:
        (u_ref, wb_ref, wc_ref, ab_ref, d_ref, wg_ref, bg_ref, x0r_ref, x0i_ref,
         o_ref, sr_ref, si_ref, xr, xi, zs, st_r, st_i) = refs
    else:
        (u_ref, wb_ref, wc_ref, ab_ref, d_ref, wg_ref, bg_ref,
         o_ref, sr_ref, si_ref, xr, xi, zs, st_r, st_i) = refs
    step = pl.program_id(0)
    S = NQ * PQ

    @pl.when(step == 0)
    def _():
        if has_x0:
            st_r[...] = x0r_ref[...]
            st_i[...] = x0i_ref[...]
        else:
            st_r[...] = jnp.zeros(st_r.shape, F32)
            st_i[...] = jnp.zeros(st_i.shape, F32)

    for q in range(NQ):
        uq = u_ref[:, q * LANES:(q + 1) * LANES].astype(BF16)
        r = jnp.dot(uq, wb_ref[q], preferred_element_type=F32)
        xr[:, q * PQ:(q + 1) * PQ] = r[:, :PQ]
        xi[:, q * PQ:(q + 1) * PQ] = r[:, PQ:]

    if R % SUBLANES == 0:
        for lc in range(S // LC):
            sl = slice(lc * LC, (lc + 1) * LC)
            ar = ab_ref[0, 0:1, sl]
            ai = ab_ref[1, 0:1, sl]
            cr, ci = st_r[:, sl], st_i[:, sl]
            for t in range(Tc):
                rs = slice(t * R, (t + 1) * R)
                nr = ar * cr - ai * ci + xr[rs, sl]
                ni = ar * ci + ai * cr + xi[rs, sl]
                xr[rs, sl] = nr
                xi[rs, sl] = ni
                cr, ci = nr, ni
            st_r[:, sl] = cr
            st_i[:, sl] = ci
    else:
        assert R * 2 == SUBLANES and Tc % 2 == 0
        lo_half = lax.broadcasted_iota(jnp.int32, (SUBLANES, LC), 0) < R
        for lc in range(S // LC):
            sl = slice(lc * LC, (lc + 1) * LC)
            ar = ab_ref[0, :, sl]
            ai = ab_ref[1, :, sl]

            def pair(j, carry, sl=sl, ar=ar, ai=ai):
                yr, yi = carry
                off = pl.multiple_of(j * SUBLANES, SUBLANES)
                br = xr[pl.ds(off, SUBLANES), sl]
                bi = xi[pl.ds(off, SUBLANES), sl]
                pr = pltpu.roll(yr, R, 0)
                pi_ = pltpu.roll(yi, R, 0)
                zr = ar * pr - ai * pi_ + br
                zi = ar * pi_ + ai * pr + bi
                qr = pltpu.roll(zr, R, 0)
                qi = pltpu.roll(zi, R, 0)
                wr = ar * qr - ai * qi + br
                wi = ar * qi + ai * qr + bi
                nr = jnp.where(lo_half, zr, wr)
                ni = jnp.where(lo_half, zi, wi)
                xr[pl.ds(off, SUBLANES), sl] = nr
                xi[pl.ds(off, SUBLANES), sl] = ni
                return nr, ni

            cr, ci = lax.fori_loop(0, Tc // 2, pair, (st_r[:, sl], st_i[:, sl]))
            st_r[:, sl] = cr
            st_i[:, sl] = ci

    for q in range(NQ):
        ps = slice(q * PQ, (q + 1) * PQ)
        ls = slice(q * LANES, (q + 1) * LANES)
        y = (jnp.dot(xr[:, ps].astype(BF16), wc_ref[0, q], preferred_element_type=F32)
             + jnp.dot(xi[:, ps].astype(BF16), wc_ref[1, q], preferred_element_type=F32))
        y = y + d_ref[:, ls] * u_ref[:, ls].astype(F32)
        zs[:, ls] = _gelu_tanh(y)
    z = zs[...]
    gate = jax.nn.sigmoid(jnp.dot(z.astype(BF16), wg_ref[...], preferred_element_type=F32) + bg_ref[...])
    o_ref[...] = (z * gate).astype(o_ref.dtype)

    @pl.when(step == pl.num_programs(0) - 1)
    def _():
        sr_ref[...] = st_r[...]
        si_ref[...] = st_i[...]


def s5_mixer(u_tm, u_block, wb, wc, ab, d, wg, bg, x0r, x0i, *, R, T, Tc, W):
    NQ, _, PQ2 = wb.shape
    PQ = PQ2 // 2
    S = NQ * PQ
    rows = Tc * R
    Rp = max(R, SUBLANES)
    LC = min(S, 512)
    in_specs = [
        pl.BlockSpec((rows, W), lambda s: (s, u_block)),
        pl.BlockSpec(wb.shape, lambda s: (0, 0, 0)),
        pl.BlockSpec(wc.shape, lambda s: (0, 0, 0, 0)),
        pl.BlockSpec(ab.shape, lambda s: (0, 0, 0)),
        pl.BlockSpec((1, W), lambda s: (0, 0)),
        pl.BlockSpec((W, W), lambda s: (0, 0)),
        pl.BlockSpec((1, W), lambda s: (0, 0)),
    ]
    args = [u_tm, wb, wc, ab, d, wg, bg]
    if x0r is not None:
        in_specs += [pl.BlockSpec((Rp, S), lambda s: (0, 0))] * 2
        args += [x0r, x0i]
    return pl.pallas_call(
        functools.partial(_s5_body, R=R, Tc=Tc, NQ=NQ, PQ=PQ, LC=LC, has_x0=x0r is not None),
        grid=(T // Tc,),
        in_specs=in_specs,
        out_specs=[pl.BlockSpec((rows, W), lambda s: (s, 0)),
                   pl.BlockSpec((Rp, S), lambda s: (0, 0)),
                   pl.BlockSpec((Rp, S), lambda s: (0, 0))],
        out_shape=[jax.ShapeDtypeStruct((T * R, W), BF16),
                   jax.ShapeDtypeStruct((Rp, S), F32),
                   jax.ShapeDtypeStruct((Rp, S), F32)],
        scratch_shapes=[pltpu.VMEM((rows, S), F32), pltpu.VMEM((rows, S), F32),
                        pltpu.VMEM((rows, W), F32),
                        pltpu.VMEM((Rp, S), F32), pltpu.VMEM((Rp, S), F32)],
        compiler_params=_params("arbitrary"),
        name="s5_mixer",
    )(*args)


def s5_weights(lam_re, lam_im, log_dt, b_re, b_im, c_re, c_im):
    G, P = lam_re.shape
    SG = b_re.shape[-1]
    NQ = G // SSM_SLAB_GROUPS
    dt = jnp.exp(log_dt.astype(F32))[:, None]
    lr, li = lam_re.astype(F32), lam_im.astype(F32)
    mag, ang = jnp.exp(lr * dt), li * dt
    ab_re, ab_im = mag * jnp.cos(ang), mag * jnp.sin(ang)
    den = lr * lr + li * li
    f_re = ((ab_re - 1.0) * lr + ab_im * li) / den
    f_im = (ab_im * lr - (ab_re - 1.0) * li) / den
    br, bi = b_re.astype(F32), b_im.astype(F32)
    bb_re = f_re[..., None] * br - f_im[..., None] * bi
    bb_im = f_re[..., None] * bi + f_im[..., None] * br
    eye = jnp.eye(SSM_SLAB_GROUPS, dtype=F32)

    def slab_in(bb):
        x = bb.reshape(NQ, SSM_SLAB_GROUPS, P, SG)
        return jnp.einsum("qgpi,gh->qgihp", x, eye).reshape(NQ, SSM_SLAB_GROUPS * SG, SSM_SLAB_GROUPS * P)

    def slab_out(cc):
        x = cc.reshape(NQ, SSM_SLAB_GROUPS, SG, P)
        return jnp.einsum("qgop,gh->qgpho", x, eye).reshape(NQ, SSM_SLAB_GROUPS * P, SSM_SLAB_GROUPS * SG)

    wb = jnp.concatenate([slab_in(bb_re), slab_in(bb_im)], axis=-1).astype(BF16)
    wc = jnp.stack([slab_out(c_re.astype(F32)), -slab_out(c_im.astype(F32))]).astype(BF16)
    ab = jnp.stack([jnp.broadcast_to(ab_re.reshape(1, G * P), (SUBLANES, G * P)),
                    jnp.broadcast_to(ab_im.reshape(1, G * P), (SUBLANES, G * P))])
    return wb, wc, ab


def _rope_128(x, cos, sin, half):
    first = lax.broadcasted_iota(jnp.int32, x.shape, 1) < half
    swapped = jnp.where(first, pltpu.roll(x, LANES - half, 1), pltpu.roll(x, half, 1))
    return x * cos + swapped * sin


def _mla_prep_body(cq_ref, ckv_ref, kr_ref, qg_ref, kvg_ref, wq_ref, wuk_ref, cos_ref, sin_ref,
                   q_ref, kcat_ref, ckvo_ref, kro_ref, *, H, NOPE, KV, ROPE, scale):
    cq = cq_ref[...].astype(F32)
    cqn = cq * lax.rsqrt(jnp.mean(cq * cq, axis=-1, keepdims=True) + RMS_EPS) * qg_ref[...]
    qall = jnp.dot(cqn.astype(BF16), wq_ref[...], preferred_element_type=F32)
    cos, sin = cos_ref[...], sin_ref[...]
    for h in range(H):
        qn = qall[:, h * NOPE:(h + 1) * NOPE]
        qr = qall[:, H * NOPE + h * LANES:H * NOPE + (h + 1) * LANES]
        q_lat = jnp.dot(qn.astype(BF16), wuk_ref[h], preferred_element_type=F32)
        q_ref[h, :, 0:KV] = (q_lat * scale).astype(q_ref.dtype)
        q_ref[h, :, KV:KV + LANES] = (_rope_128(qr, cos, sin, ROPE // 2) * scale).astype(q_ref.dtype)
    ckv = ckv_ref[...].astype(F32)
    ckvn = ckv * lax.rsqrt(jnp.mean(ckv * ckv, axis=-1, keepdims=True) + RMS_EPS) * kvg_ref[...]
    krot = _rope_128(kr_ref[...].astype(F32), cos, sin, ROPE // 2)
    ckvo_ref[...] = ckvn
    kro_ref[...] = krot[:, :ROPE]
    kcat_ref[:, 0:KV] = ckvn.astype(kcat_ref.dtype)
    kcat_ref[:, KV:KV + LANES] = krot.astype(kcat_ref.dtype)


def mla_prep(misc, qg, kvg, wq, wuk, cos, sin, *, H, QL, KV, NOPE, ROPE, bm=256):
    M = misc.shape[0]
    bm = _pick(M, bm)
    P = cos.shape[0]
    if P == M:
        tab_map = lambda i: (i, 0)
    else:
        assert P % bm == 0
        npb = P // bm
        tab_map = lambda i: (i % npb, 0)
    scale = (NOPE + ROPE) ** -0.5
    return pl.pallas_call(
        functools.partial(_mla_prep_body, H=H, NOPE=NOPE, KV=KV, ROPE=ROPE, scale=scale),
        grid=(M // bm,),
        in_specs=[
            pl.BlockSpec((bm, QL), lambda i: (i, 0)),
            pl.BlockSpec((bm, KV), lambda i: (i, QL // KV)),
            pl.BlockSpec((bm, LANES), lambda i: (i, (QL + KV) // LANES)),
            pl.BlockSpec((1, QL), lambda i: (0, 0)),
            pl.BlockSpec((1, KV), lambda i: (0, 0)),
            pl.BlockSpec(wq.shape, lambda i: (0, 0)),
            pl.BlockSpec(wuk.shape, lambda i: (0, 0, 0)),
            pl.BlockSpec((bm, LANES), tab_map),
            pl.BlockSpec((bm, LANES), tab_map),
        ],
        out_specs=[pl.BlockSpec((H, bm, KV + LANES), lambda i: (0, i, 0)),
                   pl.BlockSpec((bm, KV + LANES), lambda i: (i, 0)),
                   pl.BlockSpec((bm, KV), lambda i: (i, 0)),
                   pl.BlockSpec((bm, ROPE), lambda i: (i, 0))],
        out_shape=[jax.ShapeDtypeStruct((H, M, KV + LANES), BF16),
                   jax.ShapeDtypeStruct((M, KV + LANES), BF16),
                   jax.ShapeDtypeStruct((M, KV), F32),
                   jax.ShapeDtypeStruct((M, ROPE), F32)],
        compiler_params=_params("arbitrary"),
        name="mla_prep",
    )(misc, misc, misc, qg.reshape(1, QL), kvg.reshape(1, KV), wq, wuk, cos, sin)


def _attn_prompt_body(qi_tab, kj_tab, q_ref, k_ref, wuv_ref, o_ref, m_ref, l_ref, acc_ref,
                      *, H, bq, bk, KV, VD):
    p = pl.program_id(1)
    qi, kj = qi_tab[p], kj_tab[p]
    M = H * bq

    @pl.when(kj == 0)
    def _():
        m_ref[...] = jnp.full(m_ref.shape, -jnp.inf, F32)
        l_ref[...] = jnp.zeros(l_ref.shape, F32)
        acc_ref[...] = jnp.zeros(acc_ref.shape, F32)

    def step(masked):
        k = k_ref[...]
        if masked:
            qpos = qi * bq + lax.broadcasted_iota(jnp.int32, (bq, bk), 0)
            kpos = kj * bk + lax.broadcasted_iota(jnp.int32, (bq, bk), 1)
            keep = kpos <= qpos
        for h in range(H):
            rs = slice(h * bq, (h + 1) * bq)
            s = lax.dot_general(q_ref[h], k, NT_DIMS, preferred_element_type=F32)
            if masked:
                s = jnp.where(keep, s, -jnp.inf)
            m_old = m_ref[rs]
            m_new = jnp.maximum(m_old, jnp.max(s, axis=-1, keepdims=True))
            alpha = jnp.exp(m_old - m_new)
            pr = jnp.exp(s - m_new)
            l_ref[rs] = alpha * l_ref[rs] + jnp.sum(pr, axis=-1, keepdims=True)
            acc_ref[rs] = alpha * acc_ref[rs] + jnp.dot(pr.astype(BF16), k[:, :KV], preferred_element_type=F32)
            m_ref[rs] = m_new

    on_diag = (kj + 1) * bk - 1 > qi * bq

    @pl.when(on_diag)
    def _():
        step(True)

    @pl.when(jnp.logical_not(on_diag))
    def _():
        step(False)

    @pl.when(kj == ((qi + 1) * bq - 1) // bk)
    def _():
        o = acc_ref[...] / l_ref[...]
        for h in range(H):
            oh = o[h * bq:(h + 1) * bq].astype(BF16)
            o_ref[:, h * VD:(h + 1) * VD] = jnp.dot(oh, wuv_ref[h], preferred_element_type=F32).astype(o_ref.dtype)


def mla_attend_prompt(qcat, kcat, wuv, *, B, T, H, KV, VD, bq=128, bk=512):
    bq, bk = _pick(T, bq), _pick(T, bk)
    nq, nk = T // bq, T // bk
    pairs = [(qi, kj) for qi in range(nq) for kj in range(((qi + 1) * bq - 1) // bk + 1)]
    qi_tab = jnp.asarray(np.array([p[0] for p in pairs], np.int32))
    kj_tab = jnp.asarray(np.array([p[1] for p in pairs], np.int32))
    E = qcat.shape[-1]
    M = H * bq
    grid_spec = pltpu.PrefetchScalarGridSpec(
        num_scalar_prefetch=2,
        grid=(B, len(pairs)),
        in_specs=[
            pl.BlockSpec((H, bq, E), lambda b, p, qt, kt: (0, b * nq + qt[p], 0)),
            pl.BlockSpec((bk, E), lambda b, p, qt, kt: (b * nk + kt[p], 0)),
            pl.BlockSpec(wuv.shape, lambda b, p, qt, kt: (0, 0, 0)),
        ],
        out_specs=pl.BlockSpec((bq, H * VD), lambda b, p, qt, kt: (b * nq + qt[p], 0)),
        scratch_shapes=[pltpu.VMEM((M, 1), F32), pltpu.VMEM((M, 1), F32), pltpu.VMEM((M, KV), F32)],
    )
    return pl.pallas_call(
        functools.partial(_attn_prompt_body, H=H, bq=bq, bk=bk, KV=KV, VD=VD),
        grid_spec=grid_spec,
        out_shape=jax.ShapeDtypeStruct((B * T, H * VD), BF16),
        compiler_params=_params("arbitrary", "arbitrary"),
        name="mla_attend_prompt",
    )(qi_tab, kj_tab, qcat, kcat, wuv)


def _attn_sample_body(pt_ref, q_ref, kn_ref, ck_hbm, kr_hbm, wuv_ref, o_ref,
                      m_ref, l_ref, acc_ref, kbuf, rbuf, sem, *, layer, H, Td, PPS, PAGE, KV, ROPE, VD):
    s_id, j = pl.program_id(0), pl.program_id(1)
    nj = pl.num_programs(1)
    g = s_id * nj + j
    slot = g % 2
    q = q_ref[0]
    M = Td * H
    ql, qr = q[:, :KV], q[:, KV:KV + ROPE]

    def page_copies(seq, step, sl):
        out = []
        for i in range(PPS):
            page = pt_ref[seq, step * PPS + i]
            out.append(pltpu.make_async_copy(ck_hbm.at[layer, page], kbuf.at[sl, i], sem.at[sl]))
            out.append(pltpu.make_async_copy(kr_hbm.at[layer, page], rbuf.at[sl, i], sem.at[sl]))
        return out

    @pl.when(g == 0)
    def _():
        for c in page_copies(0, 0, 0):
            c.start()

    @pl.when(g + 1 < pl.num_programs(0) * nj)
    def _():
        nxt = j + 1
        wrap = nxt == nj
        for c in page_copies(jnp.where(wrap, s_id + 1, s_id), jnp.where(wrap, 0, nxt), 1 - slot):
            c.start()

    pltpu.make_async_copy(ck_hbm.at[layer, pl.ds(0, PPS)], kbuf.at[slot], sem.at[slot]).wait()
    pltpu.make_async_copy(kr_hbm.at[layer, pl.ds(0, PPS)], rbuf.at[slot], sem.at[slot]).wait()

    @pl.when(j == 0)
    def _():
        kn = kn_ref[0]
        TN = kn.shape[0]
        s = lax.dot_general(q, kn, NT_DIMS, preferred_element_type=F32)
        tok = lax.broadcasted_iota(jnp.int32, (M, TN), 0) // H
        col = lax.broadcasted_iota(jnp.int32, (M, TN), 1)
        s = jnp.where(col <= tok, s, -jnp.inf)
        m0 = jnp.max(s, axis=-1, keepdims=True)
        p0 = jnp.exp(s - m0)
        m_ref[...] = m0
        l_ref[...] = jnp.sum(p0, axis=-1, keepdims=True)
        acc_ref[...] = jnp.dot(p0.astype(BF16), kn[:, :KV], preferred_element_type=F32)

    kk = kbuf[slot].reshape(PPS * PAGE, KV).astype(BF16)
    rt = jnp.concatenate([rbuf[slot, i] for i in range(PPS)], axis=1).astype(BF16)
    s = (lax.dot_general(ql, kk, NT_DIMS, preferred_element_type=F32)
         + jnp.dot(qr, rt, preferred_element_type=F32))
    m_old = m_ref[...]
    m_new = jnp.maximum(m_old, jnp.max(s, axis=-1, keepdims=True))
    alpha = jnp.exp(m_old - m_new)
    pr = jnp.exp(s - m_new)
    l_ref[...] = alpha * l_ref[...] + jnp.sum(pr, axis=-1, keepdims=True)
    acc_ref[...] = alpha * acc_ref[...] + jnp.dot(pr.astype(BF16), kk, preferred_element_type=F32)
    m_ref[...] = m_new

    @pl.when(j == pl.num_programs(1) - 1)
    def _():
        o = (acc_ref[...] / l_ref[...]).astype(BF16)
        full = jnp.dot(o, wuv_ref[...], preferred_element_type=F32)
        rh = lax.broadcasted_iota(jnp.int32, (M, H * VD), 0) % H
        ch = lax.broadcasted_iota(jnp.int32, (M, H * VD), 1) // VD
        own = jnp.where(rh == ch, full, 0.0).reshape(Td, H, H * VD)
        o_ref[0] = jnp.sum(own, axis=1).astype(o_ref.dtype)


def mla_attend_sample(page_table, q_s, kn_s, cache_ckv, cache_krT, wuv_all, *, layer, H, Td, KV, ROPE, VD, pps=32):
    Bd, n_pages = page_table.shape
    PAGE = cache_ckv.shape[2]
    pps = _pick(n_pages, pps)
    E = q_s.shape[-1]
    TN = kn_s.shape[1]
    M = Td * H
    grid_spec = pltpu.PrefetchScalarGridSpec(
        num_scalar_prefetch=1,
        grid=(Bd, n_pages // pps),
        in_specs=[pl.BlockSpec((1, M, E), lambda s, j, pt: (s, 0, 0)),
                  pl.BlockSpec((1, TN, E), lambda s, j, pt: (s, 0, 0)),
                  pl.BlockSpec(memory_space=pl.ANY),
                  pl.BlockSpec(memory_space=pl.ANY),
                  pl.BlockSpec(wuv_all.shape, lambda s, j, pt: (0, 0))],
        out_specs=pl.BlockSpec((1, Td, H * VD), lambda s, j, pt: (s, 0, 0)),
        scratch_shapes=[pltpu.VMEM((M, 1), F32), pltpu.VMEM((M, 1), F32), pltpu.VMEM((M, KV), F32),
                        pltpu.VMEM((2, pps, PAGE, KV), F32), pltpu.VMEM((2, pps, ROPE, PAGE), F32),
                        pltpu.SemaphoreType.DMA((2,))],
    )
    return pl.pallas_call(
        functools.partial(_attn_sample_body, layer=layer, H=H, Td=Td, PPS=pps, PAGE=PAGE, KV=KV, ROPE=ROPE, VD=VD),
        grid_spec=grid_spec,
        out_shape=jax.ShapeDtypeStruct((Bd, Td, H * VD), BF16),
        compiler_params=_params("arbitrary", "arbitrary"),
        name="mla_attend_sample",
    )(page_table, q_s, kn_s, cache_ckv, cache_krT, wuv_all)


def _merge_body(x0, x1, x2, w0, w1, w2, g0, g1, g2, o_ref, wb0, wb1, wb2):
    @pl.when(pl.program_id(1) == 0)
    def _():
        wb0[...] = w0[...].astype(BF16)
        wb1[...] = w1[...].astype(BF16)
        wb2[...] = w2[...].astype(BF16)

    acc = g0[...].astype(F32) * jnp.dot(x0[...], wb0[...], preferred_element_type=F32)
    acc = acc + g1[...].astype(F32) * jnp.dot(x1[...], wb1[...], preferred_element_type=F32)
    acc = acc + g2[...].astype(F32) * jnp.dot(x2[...], wb2[...], preferred_element_type=F32)
    o_ref[...] = acc.astype(o_ref.dtype)


def merge_branches(xs, ws, gates, *, layer, bm=512, bn=512):
    M = xs[0].shape[0]
    Ks = [x.shape[1] for x in xs]
    N = ws[0].shape[-1]
    bm, bn = _pick(M, bm), _pick(N, bn)
    nj = N // bn
    in_specs = ([pl.BlockSpec((bm, k), lambda j, i: (i, 0)) for k in Ks]
                + [pl.BlockSpec((None, k, bn), lambda j, i: (layer, 0, j)) for k in Ks]
                + [pl.BlockSpec((bm, bn), (lambda c: (lambda j, i: (i, c * nj + j)))(c)) for c in range(3)])
    return pl.pallas_call(
        _merge_body,
        grid=(nj, M // bm),
        in_specs=in_specs,
        out_specs=pl.BlockSpec((bm, bn), lambda j, i: (i, j)),
        out_shape=jax.ShapeDtypeStruct((M, N), BF16),
        scratch_shapes=[pltpu.VMEM((k, bn), BF16) for k in Ks],
        compiler_params=_params("arbitrary", "arbitrary"),
        name="merge_branches",
    )(*xs, *ws, gates, gates, gates)


def _router_body(h_ref, rw_ref, rb_ref, c0_ref, idx_ref, w_ref, pos_ref, cnt_ref, cnt_s, *, E, NG):
    @pl.when(pl.program_id(0) == 0)
    def _():
        cnt_s[...] = c0_ref[...].astype(F32)

    logits = lax.dot_general(rw_ref[...], h_ref[...].astype(BF16), NT_DIMS, preferred_element_type=F32)
    sc = jax.nn.sigmoid(logits)
    sel = sc + rb_ref[...]
    per = E // NG
    s_rows = [sc[e:e + 1, :] for e in range(E)]
    x_rows = [sel[e:e + 1, :] for e in range(E)]
    gscore = []
    for g in range(NG):
        vals = x_rows[g * per:(g + 1) * per]
        top1 = vals[0]
        top2 = jnp.full_like(top1, -jnp.inf)
        for v in vals[1:]:
            top2 = jnp.maximum(top2, jnp.minimum(top1, v))
            top1 = jnp.maximum(top1, v)
        gscore.append(top1 + top2)
    best = jnp.zeros_like(gscore[0], dtype=jnp.int32)
    bestv = gscore[0]
    for g in range(1, NG):
        upd = gscore[g] > bestv
        best = jnp.where(upd, g, best)
        bestv = jnp.where(upd, gscore[g], bestv)
    masked = [jnp.where(best == (e // per), x_rows[e], -jnp.inf) for e in range(E)]

    def arg_top(vals):
        bv = vals[0]
        bi = jnp.zeros_like(best)
        bs = s_rows[0]
        for e in range(1, E):
            upd = vals[e] > bv
            bv = jnp.where(upd, vals[e], bv)
            bi = jnp.where(upd, e, bi)
            bs = jnp.where(upd, s_rows[e], bs)
        return bi, bs

    i1, s1 = arg_top(masked)
    i2, s2 = arg_top([jnp.where(i1 == e, -jnp.inf, masked[e]) for e in range(E)])
    tot = s1 + s2
    idx_ref[0:1, :] = i1
    idx_ref[1:2, :] = i2
    w_ref[0:1, :] = s1 / tot
    w_ref[1:2, :] = s2 / tot

    bm = i1.shape[1]
    erow = lax.broadcasted_iota(jnp.int32, (E, bm), 0)
    oh0 = jnp.where(erow == i1, 1.0, 0.0)
    oh1 = jnp.where(erow == i2, 1.0, 0.0)
    before = jnp.where(lax.broadcasted_iota(jnp.int32, (bm, bm), 0) < lax.broadcasted_iota(jnp.int32, (bm, bm), 1),
                       1.0, 0.0).astype(BF16)
    pre0 = jnp.dot(oh0.astype(BF16), before, preferred_element_type=F32)
    pre1 = jnp.dot(oh1.astype(BF16), before, preferred_element_type=F32)
    tot0 = jnp.sum(oh0, axis=1, keepdims=True)
    tot1 = jnp.sum(oh1, axis=1, keepdims=True)
    cnt = cnt_s[...]
    pos_ref[0:1, :] = jnp.sum(oh0 * (cnt + pre0), axis=0, keepdims=True).astype(jnp.int32)
    pos_ref[1:2, :] = jnp.sum(oh1 * (cnt + tot0 + pre1), axis=0, keepdims=True).astype(jnp.int32)
    cnt = cnt + tot0 + tot1
    cnt_s[...] = cnt
    cnt_ref[...] = cnt.astype(jnp.int32)


def router(h, rwT, rb, cnt0, *, bm=512):
    M, D = h.shape
    E = rwT.shape[0]
    bm = _pick(M, bm)
    return pl.pallas_call(
        functools.partial(_router_body, E=E, NG=N_EXPERT_GROUPS),
        grid=(M // bm,),
        in_specs=[pl.BlockSpec((bm, D), lambda i: (i, 0)),
                  pl.BlockSpec((E, D), lambda i: (0, 0)),
                  pl.BlockSpec((E, 1), lambda i: (0, 0)),
                  pl.BlockSpec((E, 1), lambda i: (0, 0))],
        out_specs=[pl.BlockSpec((TOP_K, bm), lambda i: (0, i)),
                   pl.BlockSpec((TOP_K, bm), lambda i: (0, i)),
                   pl.BlockSpec((TOP_K, bm), lambda i: (0, i)),
                   pl.BlockSpec((E, 1), lambda i: (0, 0))],
        out_shape=[jax.ShapeDtypeStruct((TOP_K, M), jnp.int32),
                   jax.ShapeDtypeStruct((TOP_K, M), F32),
                   jax.ShapeDtypeStruct((TOP_K, M), jnp.int32),
                   jax.ShapeDtypeStruct((E, 1), jnp.int32)],
        scratch_shapes=[pltpu.VMEM((E, 1), F32)],
        compiler_params=_params("arbitrary"),
        name="moe_router",
    )(h, rwT, rb, cnt0)


def _moe_body(te_ref, nv_ref, x_ref, wgu_ref, wd_ref, y_ref, *, F):
    del te_ref
    i = pl.program_id(0)

    @pl.when(nv_ref[i] > 0)
    def _():
        x = x_ref[...].astype(BF16)
        ab = jnp.dot(x, wgu_ref[...], preferred_element_type=F32)
        t = (_silu(ab[:, :F]) * ab[:, F:]).astype(BF16)
        y_ref[...] = jnp.dot(t, wd_ref[...], preferred_element_type=F32)

    @pl.when(nv_ref[i] == 0)
    def _():
        y_ref[...] = jnp.zeros(y_ref.shape, F32)


def moe_experts(xs, wgu, wd, tile_expert, tile_nvalid, *, layer, bm):
    n_slots, D = xs.shape
    F2 = wgu.shape[-1]
    F = F2 // 2
    grid_spec = pltpu.PrefetchScalarGridSpec(
        num_scalar_prefetch=2,
        grid=(n_slots // bm,),
        in_specs=[
            pl.BlockSpec((bm, D), lambda i, te, nv: (i, 0)),
            pl.BlockSpec((None, None, D, F2), lambda i, te, nv: (layer, te[i], 0, 0)),
            pl.BlockSpec((None, None, F, D), lambda i, te, nv: (layer, te[i], 0, 0)),
        ],
        out_specs=pl.BlockSpec((bm, D), lambda i, te, nv: (i, 0)),
    )
    return pl.pallas_call(
        functools.partial(_moe_body, F=F),
        grid_spec=grid_spec,
        out_shape=jax.ShapeDtypeStruct((n_slots, D), F32),
        compiler_params=_params("arbitrary"),
        name="moe_experts",
    )(tile_expert, tile_nvalid, xs, wgu, wd)


def moe_plan(counts, idx_pos, *, E, bm, n_slots):
    counts = counts.reshape(E)
    padded = ((counts + bm - 1) // bm) * bm
    ends = jnp.cumsum(padded)
    offs = ends - padded
    eids = jnp.arange(E, dtype=jnp.int32)
    slots = [jnp.sum(jnp.where(idx[..., None] == eids, offs, 0), axis=-1) + pos for idx, pos in idx_pos]
    tile_start = jnp.arange(n_slots // bm, dtype=jnp.int32) * bm
    tile_expert = jnp.minimum(jnp.sum((tile_start[:, None] >= ends[None, :]).astype(jnp.int32), axis=1), E - 1)
    first = jnp.sum(jnp.where(tile_expert[:, None] == eids, offs + counts, 0), axis=-1)
    tile_nvalid = jnp.clip(first - tile_start, 0, bm).astype(jnp.int32)
    return tile_expert.astype(jnp.int32), tile_nvalid, slots


def _dispatch_body(slot_ref, h_ref, xs_in, xs_out, sem, *, bt):
    del xs_in

    def row_copy(r, k):
        return pltpu.make_async_copy(h_ref.at[pl.ds(r, 1)], xs_out.at[pl.ds(slot_ref[k, r], 1)], sem)

    def start(r, c):
        for k in range(TOP_K):
            row_copy(r, k).start()
        return c

    lax.fori_loop(0, bt, start, 0, unroll=8)
    for k in range(TOP_K):
        pltpu.make_async_copy(h_ref, xs_out.at[pl.ds(0, bt)], sem).wait()


def moe_dispatch(h, slot, xs, *, bt=256):
    M, D = h.shape
    bt = _pick(M, bt)
    return pl.pallas_call(
        functools.partial(_dispatch_body, bt=bt),
        grid=(M // bt,),
        in_specs=[pl.BlockSpec((TOP_K, bt), lambda i: (0, i), memory_space=pltpu.SMEM),
                  pl.BlockSpec((bt, D), lambda i: (i, 0)),
                  pl.BlockSpec(memory_space=pl.ANY)],
        out_specs=pl.BlockSpec(memory_space=pl.ANY),
        out_shape=jax.ShapeDtypeStruct(xs.shape, xs.dtype),
        scratch_shapes=[pltpu.SemaphoreType.DMA(())],
        input_output_aliases={2: 0},
        compiler_params=_params("arbitrary"),
        name="moe_dispatch",
    )(slot, h, xs)


def _combine_body(*refs, mode, final, bt):
    it = iter(refs)
    slot_ref, x_ref, w_ref, g_ref = next(it), next(it), next(it), next(it)
    ng_ref = next(it) if final else None
    ys_hbm, o_ref, ybuf, sem = next(it), next(it), next(it), next(it)

    def start(r, c):
        for k in range(TOP_K):
            pltpu.make_async_copy(ys_hbm.at[pl.ds(slot_ref[k, r], 1)], ybuf.at[k, pl.ds(r, 1)], sem).start()
        return c

    lax.fori_loop(0, bt, start, 0, unroll=8)
    for k in range(TOP_K):
        pltpu.make_async_copy(ys_hbm.at[pl.ds(0, bt)], ybuf.at[k], sem).wait()
    w = w_ref[...]
    y = w[:, 0:1] * ybuf[0] + w[:, 1:2] * ybuf[1]
    g = g_ref[0] if mode == "group" else g_ref[...]
    x = x_ref[...] + g * y
    if final:
        x = x * lax.rsqrt(jnp.mean(x * x, axis=-1, keepdims=True) + RMS_EPS) * ng_ref[...]
    o_ref[...] = x


def moe_combine(x, ys, slot, w, mod, *, gate_chunk, rows_per_group=None, final_g=None, bt=256):
    M, D = x.shape
    bt = _pick(rows_per_group or M, bt)
    in_specs = [pl.BlockSpec((TOP_K, bt), lambda i: (0, i), memory_space=pltpu.SMEM),
                pl.BlockSpec((bt, D), lambda i: (i, 0)),
                pl.BlockSpec((bt, TOP_K), lambda i: (i, 0))]
    if rows_per_group is not None:
        mode = "group"
        rpg = rows_per_group // bt
        in_specs.append(pl.BlockSpec((1, 1, D), lambda i: (i // rpg, 0, gate_chunk)))
    else:
        mode = "token"
        in_specs.append(pl.BlockSpec((bt, D), lambda i: (i, gate_chunk)))
    args = [slot, x, w, mod]
    if final_g is not None:
        in_specs.append(pl.BlockSpec((1, D), lambda i: (0, 0)))
        args.append(final_g.reshape(1, D))
    in_specs.append(pl.BlockSpec(memory_space=pl.ANY))
    args.append(ys)
    return pl.pallas_call(
        functools.partial(_combine_body, mode=mode, final=final_g is not None, bt=bt),
        grid=(M // bt,),
        in_specs=in_specs,
        out_specs=pl.BlockSpec((bt, D), lambda i: (i, 0)),
        out_shape=jax.ShapeDtypeStruct((M, D), F32),
        scratch_shapes=[pltpu.VMEM((TOP_K, bt, D), F32), pltpu.SemaphoreType.DMA(())],
        compiler_params=_params("arbitrary"),
        name="moe_combine",
    )(*args)


def _rope_tables(pos, rope_dim):
    half = rope_dim // 2
    inv = ROPE_THETA ** (-jnp.arange(half, dtype=F32) / half)
    ang = pos.astype(F32)[:, None] * inv[None, :]
    cos, sin = jnp.cos(ang), jnp.sin(ang)
    pad = jnp.zeros((pos.shape[0], LANES - rope_dim), F32)
    return (jnp.concatenate([cos, cos, pad], axis=1), jnp.concatenate([-sin, sin, pad], axis=1))


def _pad_cols(w, n):
    return jnp.pad(w, ((0, 0), (0, n - w.shape[1])))


def kernel(x_prompt, x_sample, cache_ckv, cache_krope, state_gla, state_ssm_re, state_ssm_im,
           page_table, c_prompt, c_sample, norm_mix, norm_ffn, norm_final, w_ada, b_ada, w_in,
           gla_w_a2, gla_b_a2, gla_norm, ssm_lambda_re, ssm_lambda_im, ssm_log_dt, ssm_b_re,
           ssm_b_im, ssm_c_re, ssm_c_im, ssm_d, ssm_w_glu, ssm_b_glu, mla_q_norm, mla_kv_norm,
           mla_w_uq, mla_w_uk, mla_w_uv, w_br_gla, w_br_ssm, w_br_mla, w_out, router_w,
           router_bias, moe_w_gu, moe_w_down):
    B, T, D = x_prompt.shape
    Bd, Td = x_sample.shape[:2]
    depth = w_in.shape[0]
    _, _, H, DK, DV = state_gla.shape
    QK, VW = H * DK, H * DV
    RANK = gla_w_a2.shape[1]
    G, P = ssm_lambda_re.shape[1:]
    SW = ssm_d.shape[1]
    QL = mla_q_norm.shape[1]
    KV, MH, NOPE = mla_w_uk.shape[1:]
    ROPE = cache_krope.shape[-1]
    VD = mla_w_uv.shape[-1]
    E = router_w.shape[1]
    F = moe_w_down.shape[2]
    n_pages = page_table.shape[1]
    PAGE = cache_ckv.shape[2]
    past_len = n_pages * PAGE
    Np, Ns = B * T, Bd * Td
    n_all = Np + Ns
    Tdp = SUBLANES
    TN = 16

    o_q, o_k, o_v, o_g = 0, QK, 2 * QK, 2 * QK + VW
    o_a = o_g + VW
    o_u = o_a + RANK
    o_cq = o_u + SW
    o_ckv = o_cq + QL
    o_kr = o_ckv + KV
    o_gt = o_kr + ROPE

    cos_p, sin_p = _rope_tables(jnp.arange(T, dtype=jnp.int32), ROPE)
    cos_s, sin_s = _rope_tables(past_len + jnp.arange(Td, dtype=jnp.int32), ROPE)
    cos_s, sin_s = jnp.tile(cos_s, (Bd, 1)), jnp.tile(sin_s, (Bd, 1))

    c_all = jnp.concatenate([c_prompt, c_sample], axis=0)
    rwT = router_w.T.astype(BF16)
    rb = router_bias.astype(F32).reshape(E, 1)
    moe_bm = 256
    n_slots = ((TOP_K * n_all + E * (moe_bm - 1)) // moe_bm) * moe_bm
    wgu_b = moe_w_gu.astype(BF16)
    wd_b = moe_w_down.astype(BF16)
    cache_krT = jnp.swapaxes(cache_krope, 2, 3)

    xp = x_prompt.reshape(Np, D)
    xs = x_sample.reshape(Ns, D)
    outs = {k: [] for k in ("ckv_p", "kr_p", "ckv_s", "kr_s", "gla_p", "gla_s",
                            "sre_p", "sim_p", "sre_s", "sim_s")}
    y_p = y_s = None
    for l in range(depth):
        wl = w_in[l]
        w_qkvg = wl[:, :o_a]
        w_misc = jnp.concatenate([wl[:, o_cq:o_kr], _pad_cols(wl[:, o_kr:o_gt], LANES),
                                  _pad_cols(wl[:, o_a:o_u], LANES)], axis=1)
        misc_w = w_misc.shape[1]
        w_u = wl[:, o_u:o_cq]
        w_gt = wl[:, o_gt:]
        alr_block = (QL + KV + LANES) // LANES
        wa = jnp.pad(gla_w_a2[l], ((0, LANES - RANK), (0, 0))).astype(BF16)
        ba = gla_b_a2[l].astype(F32).reshape(1, QK)
        ng = gla_norm[l].astype(F32).reshape(1, DV)
        wb, wc, ab = s5_weights(ssm_lambda_re[l], ssm_lambda_im[l], ssm_log_dt[l], ssm_b_re[l],
                                ssm_b_im[l], ssm_c_re[l], ssm_c_im[l])
        sd = ssm_d[l].astype(F32).reshape(1, SW)
        wglu = ssm_w_glu[l].astype(BF16)
        bglu = ssm_b_glu[l].astype(F32).reshape(1, SW)
        wq3 = mla_w_uq[l].reshape(QL, MH, NOPE + ROPE)
        wq = jnp.concatenate([wq3[:, :, :NOPE].reshape(QL, MH * NOPE),
                              jnp.pad(wq3[:, :, NOPE:], ((0, 0), (0, 0), (0, LANES - ROPE))).reshape(QL, MH * LANES)],
                             axis=1).astype(BF16)
        wuk = jnp.transpose(mla_w_uk[l], (1, 2, 0)).astype(BF16)
        wuv = jnp.transpose(mla_w_uv[l], (1, 0, 2)).astype(BF16)
        wuv_all = mla_w_uv[l].reshape(KV, MH * VD).astype(BF16)

        mod = matmul(c_all, w_ada, layer=l, bias=b_ada[l], pre="silu", name="ada_mod")
        mod_p = mod[:B].reshape(B, 1, N_MOD * D)
        mod_s = jnp.repeat(mod[B:], Td, axis=0)

        def mixers(x, *, prompt):
            nb, nt = (B, T) if prompt else (Bd, Td)
            n = nb * nt
            rpg = T if prompt else None
            m = mod_p if prompt else mod_s
            h = norm_mod(x, norm_mix[l], m, sc_chunk=1, sh_chunk=0, rows_per_group=rpg)
            qkvg = matmul(h, w_qkvg, out_dtype=BF16, name="in_qkvg")
            misc = matmul(h, w_misc, out_dtype=F32, name="in_misc")
            u = matmul(h, w_u, out_dtype=F32, name="in_u")
            gates = matmul(h, w_gt, epilogue="sigmoid", out_dtype=BF16, name="in_gates")
            if prompt:
                o_gla, gla_s = gla_mixer(qkvg.reshape(nb, nt, -1), misc.reshape(nb, nt, -1), wa, ba, ng, None,
                                         B=nb, T=nt, H=H, DK=DK, DV=DV, C=_pick(nt, 128), t_valid=_pick(nt, 128),
                                         alr_block=alr_block, Bb=2 if nb % 2 == 0 else 1)
                o_gla = o_gla.reshape(n, VW)
            else:
                padt = ((0, 0), (0, Tdp - nt), (0, 0))
                o_gla, gla_s = gla_mixer(jnp.pad(qkvg.reshape(nb, nt, -1), padt),
                                         jnp.pad(misc.reshape(nb, nt, -1), padt), wa, ba, ng, state_gla[l],
                                         B=nb, T=Tdp, H=H, DK=DK, DV=DV, C=Tdp, t_valid=nt,
                                         alr_block=alr_block, Bb=_pick(nb, 4))
                o_gla = o_gla[:, :nt].reshape(n, VW)
            u_tm = u.reshape(nb, nt, SW).transpose(1, 0, 2).reshape(n, SW)
            if prompt:
                o_ssm, s_re, s_im = s5_mixer(u_tm, 0, wb, wc, ab, sd, wglu, bglu, None, None,
                                             R=nb, T=nt, Tc=_pick(nt, 64), W=SW)
                s_re, s_im = s_re[nb:2 * nb], s_im[nb:2 * nb]
            else:
                o_ssm, s_re, s_im = s5_mixer(u_tm, 0, wb, wc, ab, sd, wglu, bglu,
                                             state_ssm_re[l].reshape(nb, G * P), state_ssm_im[l].reshape(nb, G * P),
                                             R=nb, T=nt, Tc=_pick(nt, 2), W=SW)
            o_ssm = o_ssm.reshape(nt, nb, SW).transpose(1, 0, 2).reshape(n, SW)
            cos, sin = (cos_p, sin_p) if prompt else (cos_s, sin_s)
            qcat, kcat, ckv_n, kr_r = mla_prep(misc, mla_q_norm[l], mla_kv_norm[l], wq, wuk, cos, sin,
                                               H=MH, QL=QL, KV=KV, NOPE=NOPE, ROPE=ROPE)
            if prompt:
                o_mla = mla_attend_prompt(qcat, kcat, wuv, B=nb, T=nt, H=MH, KV=KV, VD=VD)
            else:
                q_s = qcat.reshape(MH, nb, nt, -1).transpose(1, 2, 0, 3).reshape(nb, nt * MH, -1)
                kn_s = jnp.pad(kcat.reshape(nb, nt, -1), ((0, 0), (0, TN - nt), (0, 0)))
                o_mla = mla_attend_sample(page_table, q_s, kn_s, cache_ckv, cache_krT, wuv_all, layer=l,
                                          H=MH, Td=nt, KV=KV, ROPE=ROPE, VD=VD).reshape(n, MH * VD)
            merged = merge_branches([o_gla, o_ssm, o_mla], [w_br_gla, w_br_ssm, w_br_mla], gates, layer=l)
            x = matmul(merged, w_out, layer=l, res=x, gate=m, gate_chunk=2, rows_per_group=rpg, name="out_proj")
            hf = norm_mod(x, norm_ffn[l], m, sc_chunk=4, sh_chunk=3, rows_per_group=rpg, out_dtype=F32,
                          name="norm_ffn")
            state = (ckv_n.reshape(nb, nt, KV), kr_r.reshape(nb, nt, ROPE), gla_s,
                     s_re.reshape(nb, G, P), s_im.reshape(nb, G, P))
            return x, hf, state

        xp, hf_p, st_p = mixers(xp, prompt=True)
        xs, hf_s, st_s = mixers(xs, prompt=False)
        for k, v in zip(("ckv_p", "kr_p", "gla_p", "sre_p", "sim_p"), st_p):
            outs[k].append(v)
        for k, v in zip(("ckv_s", "kr_s", "gla_s", "sre_s", "sim_s"), st_s):
            outs[k].append(v)

        idx_p, wts_p, pos_p, cnt = router(hf_p, rwT, rb, jnp.zeros((E, 1), jnp.int32))
        idx_s, wts_s, pos_s, cnt = router(hf_s, rwT, rb, cnt)
        tile_expert, tile_nvalid, (slot_p, slot_s) = moe_plan(
            cnt, [(idx_p, pos_p), (idx_s, pos_s)], E=E, bm=moe_bm, n_slots=n_slots)
        x_slots = moe_dispatch(hf_p, slot_p, jnp.zeros((n_slots, D), F32))
        x_slots = moe_dispatch(hf_s, slot_s, x_slots)
        y_slots = moe_experts(x_slots, wgu_b, wd_b, tile_expert, tile_nvalid, layer=l, bm=moe_bm)
        last = l == depth - 1
        fg = norm_final if last else None
        xp = moe_combine(xp, y_slots, slot_p, wts_p.T, mod_p, gate_chunk=5, rows_per_group=T, final_g=fg)
        xs = moe_combine(xs, y_slots, slot_s, wts_s.T, mod_s, gate_chunk=5, final_g=fg)
        if last:
            y_p, y_s = xp, xs

    st = lambda k: jnp.stack(outs[k])
    return (y_p.reshape(B, T, D), y_s.reshape(Bd, Td, D),
            st("ckv_p"), st("kr_p"), st("ckv_s"), st("kr_s"), st("gla_p"), st("gla_s"),
            st("sre_p"), st("sim_p"), st("sre_s"), st("sim_s"))
```

```python
import functools
import math

import jax
import jax.numpy as jnp
import numpy as np
from jax import lax
from jax.experimental import pallas as pl
from jax.experimental.pallas import tpu as pltpu

F32 = jnp.float32
BF16 = jnp.bfloat16

GLA_TAU = 16.0
ROPE_THETA = 10000.0
RMS_EPS = 1e-6
N_EXPERT_GROUPS = 4
TOP_K = 2
N_MOD = 6
N_BRANCH = 3
SSM_SLAB_GROUPS = 8

VMEM_LIMIT_BYTES = 52 * 1024 * 1024
LANES = 128
SUBLANES = 8

NT_DIMS = (((1,), (1,)), ((), ()))


def _params(*sem):
    return pltpu.CompilerParams(dimension_semantics=sem, vmem_limit_bytes=VMEM_LIMIT_BYTES)


def _pick(n, pref):
    if n <= pref:
        return n
    t = pref
    while n % t:
        t //= 2
    return t


def _silu(x):
    return x * jax.nn.sigmoid(x)


def _mm_body(*refs, pre, epilogue, has_bias, res_mode):
    it = iter(refs)
    x_ref, w_ref = next(it), next(it)
    b_ref = next(it) if has_bias else None
    r_ref = g_ref = None
    if res_mode:
        r_ref, g_ref = next(it), next(it)
    o_ref, wb_ref = next(it), next(it)

    @pl.when(pl.program_id(1) == 0)
    def _():
        wb_ref[...] = w_ref[...].astype(BF16)

    x = x_ref[...]
    if pre == "silu":
        x = _silu(x.astype(F32))
    elif pre == "swiglu":
        f = x.shape[1] // 2
        xf = x.astype(F32)
        x = _silu(xf[:, :f]) * xf[:, f:]
    acc = jnp.dot(x.astype(BF16), wb_ref[...], preferred_element_type=F32)
    if has_bias:
        acc = acc + b_ref[...]
    if epilogue == "sigmoid":
        acc = jax.nn.sigmoid(acc)
    if res_mode == "group":
        acc = r_ref[...] + g_ref[0] * acc
    elif res_mode == "token":
        acc = r_ref[...] + g_ref[...] * acc
    o_ref[...] = acc.astype(o_ref.dtype)


def matmul(x, w, *, bias=None, pre=None, epilogue=None, out_dtype=F32, bm=512, bn=1024,
           res=None, gate=None, gate_chunk=0, rows_per_group=None, layer=None, name="matmul"):
    M, K = x.shape
    N = w.shape[-1]
    bm, bn = _pick(rows_per_group or M, bm), _pick(N, bn)
    nj = N // bn
    kw = K // 2 if pre == "swiglu" else K
    assert w.shape[-2] == kw
    if layer is None:
        w_spec = pl.BlockSpec((kw, bn), lambda j, i: (0, j))
    else:
        w_spec = pl.BlockSpec((None, kw, bn), lambda j, i: (layer, 0, j))
    in_specs = [pl.BlockSpec((bm, K), lambda j, i: (i, 0)), w_spec]
    args = [x, w]
    if bias is not None:
        in_specs.append(pl.BlockSpec((1, bn), lambda j, i: (0, j)))
        args.append(bias.reshape(1, N))
    res_mode = None
    if res is not None:
        in_specs.append(pl.BlockSpec((bm, bn), lambda j, i: (i, j)))
        args.append(res)
        if rows_per_group is not None:
            res_mode = "group"
            assert rows_per_group % bm == 0
            rpg = rows_per_group // bm
            in_specs.append(pl.BlockSpec((1, 1, bn), lambda j, i: (i // rpg, 0, gate_chunk * nj + j)))
        else:
            res_mode = "token"
            in_specs.append(pl.BlockSpec((bm, bn), lambda j, i: (i, gate_chunk * nj + j)))
        args.append(gate)
    return pl.pallas_call(
        functools.partial(_mm_body, pre=pre, epilogue=epilogue, has_bias=bias is not None,
                          res_mode=res_mode),
        grid=(nj, M // bm),
        in_specs=in_specs,
        out_specs=pl.BlockSpec((bm, bn), lambda j, i: (i, j)),
        out_shape=jax.ShapeDtypeStruct((M, N), out_dtype),
        scratch_shapes=[pltpu.VMEM((kw, bn), BF16)],
        compiler_params=_params("arbitrary", "arbitrary"),
        name=name,
    )(*args)


def _norm_body(*refs, mode):
    if mode:
        x_ref, g_ref, sc_ref, sh_ref, o_ref = refs
    else:
        x_ref, g_ref, o_ref = refs
    x = x_ref[...].astype(F32)
    y = x * lax.rsqrt(jnp.mean(x * x, axis=-1, keepdims=True) + RMS_EPS) * g_ref[...]
    if mode == "group":
        y = y * (1.0 + sc_ref[0]) + sh_ref[0]
    elif mode == "token":
        y = y * (1.0 + sc_ref[...]) + sh_ref[...]
    o_ref[...] = y.astype(o_ref.dtype)


def norm_mod(x, g, mod=None, *, sc_chunk=0, sh_chunk=0, rows_per_group=None, out_dtype=BF16,
             bm=512, name="norm_mod"):
    M, D = x.shape
    bm = _pick(rows_per_group or M, bm)
    in_specs = [pl.BlockSpec((bm, D), lambda i: (i, 0)), pl.BlockSpec((1, D), lambda i: (0, 0))]
    args = [x, g.reshape(1, D)]
    mode = None
    if mod is not None:
        if rows_per_group is not None:
            mode = "group"
            rpg = rows_per_group // bm
            in_specs += [pl.BlockSpec((1, 1, D), lambda i: (i // rpg, 0, sc_chunk)),
                         pl.BlockSpec((1, 1, D), lambda i: (i // rpg, 0, sh_chunk))]
        else:
            mode = "token"
            in_specs += [pl.BlockSpec((bm, D), lambda i: (i, sc_chunk)),
                         pl.BlockSpec((bm, D), lambda i: (i, sh_chunk))]
        args += [mod, mod]
    return pl.pallas_call(
        functools.partial(_norm_body, mode=mode),
        grid=(M // bm,),
        in_specs=in_specs,
        out_specs=pl.BlockSpec((bm, D), lambda i: (i, 0)),
        out_shape=jax.ShapeDtypeStruct((M, D), out_dtype),
        compiler_params=_params("arbitrary"),
        name=name,
    )(*args)


def _log_sigmoid(z):
    return jnp.minimum(z, 0.0) - jnp.log1p(jnp.exp(-jnp.abs(z)))


def _gla_body(*refs, H, DK, DV, C, t_valid, has_s0, Bb):
    if has_s0:
        q_ref, k_ref, v_ref, g_ref, a_ref, wa_ref, ba_ref, ng_ref, s0_ref, o_ref, st_ref, sT_ref = refs
    else:
        q_ref, k_ref, v_ref, g_ref, a_ref, wa_ref, ba_ref, ng_ref, o_ref, st_ref, sT_ref = refs
    c = pl.program_id(1)

    @pl.when(c == 0)
    def _():
        for bb in range(Bb):
            for h in range(H):
                if has_s0:
                    sT_ref[bb * H + h] = s0_ref[bb, h].T
                else:
                    sT_ref[bb * H + h] = jnp.zeros((DV, DK), F32)

    row = lax.broadcasted_iota(jnp.int32, (C, 1), 0)
    tri = row >= lax.broadcasted_iota(jnp.int32, (1, C), 1)
    trib = jnp.where(tri, 1.0, 0.0).astype(BF16)
    live = row < t_valid
    mid = C // 2 - 1
    for bb, h in [(bb, h) for bb in range(Bb) for h in range(H)]:
        alr = a_ref[bb].astype(BF16)
        ks = slice(h * DK, (h + 1) * DK)
        vs = slice(h * DV, (h + 1) * DV)
        z = jnp.dot(alr, wa_ref[:, ks], preferred_element_type=F32) + ba_ref[:, ks]
        la = _log_sigmoid(z) * (1.0 / GLA_TAU)
        if t_valid < C:
            la = jnp.where(live, la, 0.0)
        hi = la.astype(BF16)
        lo = (la - hi.astype(F32)).astype(BF16)
        b = (jnp.dot(trib, hi, preferred_element_type=F32)
             + jnp.dot(trib, lo, preferred_element_type=F32))
        b_mid = b[mid:mid + 1, :]
        b_last = b[C - 1:C, :]
        q = q_ref[bb, :, ks].astype(F32) * (DK ** -0.5)
        k = k_ref[bb, :, ks].astype(F32)
        if t_valid < C:
            k = jnp.where(live, k, 0.0)
        v = v_ref[bb, :, vs].astype(BF16)
        qe = (q * jnp.exp(b - b_mid)).astype(BF16)
        ke = (k * jnp.exp(b_mid - b)).astype(BF16)
        a = lax.dot_general(qe, ke, NT_DIMS, preferred_element_type=F32)
        a = jnp.where(tri, a, 0.0)
        qb = (q * jnp.exp(b)).astype(BF16)
        kl = (k * jnp.exp(b_last - b)).astype(BF16)
        sT = sT_ref[bb * H + h]
        o = (jnp.dot(a.astype(BF16), v, preferred_element_type=F32)
             + lax.dot_general(qb, sT.astype(BF16), NT_DIMS, preferred_element_type=F32))
        vT = v_ref[bb, :, vs].astype(F32).T.astype(BF16)
        sT_ref[bb * H + h] = jnp.exp(b_last) * sT + jnp.dot(vT, kl, preferred_element_type=F32)
        on = o * lax.rsqrt(jnp.mean(o * o, axis=-1, keepdims=True) + RMS_EPS) * ng_ref[...]
        gg = g_ref[bb, :, vs].astype(F32)
        o_ref[bb, :, vs] = (on * _silu(gg)).astype(o_ref.dtype)

    @pl.when(c == pl.num_programs(1) - 1)
    def _():
        for bb in range(Bb):
            for h in range(H):
                st_ref[bb, h] = sT_ref[bb * H + h].T


def gla_mixer(qkvg, misc, wa, ba, ng, s0, *, B, T, H, DK, DV, C, t_valid, alr_block, Bb=1):
    QK, VW = H * DK, H * DV
    nc = T // C
    assert B % Bb == 0
    in_specs = [
        pl.BlockSpec((Bb, C, QK), lambda b, c: (b, c, 0)),
        pl.BlockSpec((Bb, C, QK), lambda b, c: (b, c, 1)),
        pl.BlockSpec((Bb, C, VW), lambda b, c: (b, c, (2 * QK) // VW)),
        pl.BlockSpec((Bb, C, VW), lambda b, c: (b, c, (2 * QK) // VW + 1)),
        pl.BlockSpec((Bb, C, LANES), lambda b, c: (b, c, alr_block)),
        pl.BlockSpec((LANES, QK), lambda b, c: (0, 0)),
        pl.BlockSpec((1, QK), lambda b, c: (0, 0)),
        pl.BlockSpec((1, DV), lambda b, c: (0, 0)),
    ]
    args = [qkvg, qkvg, qkvg, qkvg, misc, wa, ba, ng]
    if s0 is not None:
        in_specs.append(pl.BlockSpec((Bb, H, DK, DV), lambda b, c: (b, 0, 0, 0)))
        args.append(s0)
    assert (2 * QK) % VW == 0
    return pl.pallas_call(
        functools.partial(_gla_body, H=H, DK=DK, DV=DV, C=C, t_valid=t_valid, has_s0=s0 is not None, Bb=Bb),
        grid=(B // Bb, nc),
        in_specs=in_specs,
        out_specs=[pl.BlockSpec((Bb, C, VW), lambda b, c: (b, c, 0)),
                   pl.BlockSpec((Bb, H, DK, DV), lambda b, c: (b, 0, 0, 0))],
        out_shape=[jax.ShapeDtypeStruct((B, T, VW), BF16),
                   jax.ShapeDtypeStruct((B, H, DK, DV), F32)],
        scratch_shapes=[pltpu.VMEM((Bb * H, DV, DK), F32)],
        compiler_params=_params("arbitrary", "arbitrary"),
        name="gla_mixer",
    )(*args)


def _gelu_tanh(y):
    return 0.5 * y * (1.0 + jnp.tanh(math.sqrt(2.0 / math.pi) * (y + 0.044715 * (y * y * y))))


def _s5_body(*refs, R, Tc, NQ, PQ, LC, has_x0):
    if has_x0:
        (u_ref, wb_ref, wc_ref, ab_ref, d_ref, wg_ref, bg_ref, x0r_ref, x0i_ref,
         o_ref, sr_ref, si_ref, xr, xi, zs, st_r, st_i) = refs
    else:
        (u_ref, wb_ref, wc_ref, ab_ref, d_ref, wg_ref, bg_ref,
         o_ref, sr_ref, si_ref, xr, xi, zs, st_r, st_i) = refs
    step = pl.program_id(0)
    S = NQ * PQ

    @pl.when(step == 0)
    def _():
        if has_x0:
            st_r[...] = x0r_ref[...]
            st_i[...] = x0i_ref[...]
        else:
            st_r[...] = jnp.zeros(st_r.shape, F32)
            st_i[...] = jnp.zeros(st_i.shape, F32)

    for q in range(NQ):
        uq = u_ref[:, q * LANES:(q + 1) * LANES].astype(BF16)
        r = jnp.dot(uq, wb_ref[q], preferred_element_type=F32)
        xr[:, q * PQ:(q + 1) * PQ] = r[:, :PQ]
        xi[:, q * PQ:(q + 1) * PQ] = r[:, PQ:]

    if R % SUBLANES == 0:
        for lc in range(S // LC):
            sl = slice(lc * LC, (lc + 1) * LC)
            ar = ab_ref[0, 0:1, sl]
            ai = ab_ref[1, 0:1, sl]
            cr, ci = st_r[:, sl], st_i[:, sl]
            for t in range(Tc):
                rs = slice(t * R, (t + 1) * R)
                nr = ar * cr - ai * ci + xr[rs, sl]
                ni = ar * ci + ai * cr + xi[rs, sl]
                xr[rs, sl] = nr
                xi[rs, sl] = ni
                cr, ci = nr, ni
            st_r[:, sl] = cr
            st_i[:, sl] = ci
    else:
        assert R * 2 == SUBLANES and Tc % 2 == 0
        lo_half = lax.broadcasted_iota(jnp.int32, (SUBLANES, LC), 0) < R
        for lc in range(S // LC):
            sl = slice(lc * LC, (lc + 1) * LC)
            ar = ab_ref[0, :, sl]
            ai = ab_ref[1, :, sl]

            def pair(j, carry, sl=sl, ar=ar, ai=ai):
                yr, yi = carry
                off = pl.multiple_of(j * SUBLANES, SUBLANES)
                br = xr[pl.ds(off, SUBLANES), sl]
                bi = xi[pl.ds(off, SUBLANES), sl]
                pr = pltpu.roll(yr, R, 0)
                pi_ = pltpu.roll(yi, R, 0)
                zr = ar * pr - ai * pi_ + br
                zi = ar * pi_ + ai * pr + bi
                qr = pltpu.roll(zr, R, 0)
                qi = pltpu.roll(zi, R, 0)
                wr = ar * qr - ai * qi + br
                wi = ar * qi + ai * qr + bi
                nr = jnp.where(lo_half, zr, wr)
                ni = jnp.where(lo_half, zi, wi)
                xr[pl.ds(off, SUBLANES), sl] = nr
                xi[pl.ds(off, SUBLANES), sl] = ni
                return nr, ni

            cr, ci = lax.fori_loop(0, Tc // 2, pair, (st_r[:, sl], st_i[:, sl]))
            st_r[:, sl] = cr
            st_i[:, sl] = ci

    for q in range(NQ):
        ps = slice(q * PQ, (q + 1) * PQ)
        ls = slice(q * LANES, (q + 1) * LANES)
        y = (jnp.dot(xr[:, ps].astype(BF16), wc_ref[0, q], preferred_element_type=F32)
             + jnp.dot(xi[:, ps].astype(BF16), wc_ref[1, q], preferred_element_type=F32))
        y = y + d_ref[:, ls] * u_ref[:, ls].astype(F32)
        zs[:, ls] = _gelu_tanh(y)
    z = zs[...]
    gate = jax.nn.sigmoid(jnp.dot(z.astype(BF16), wg_ref[...], preferred_element_type=F32) + bg_ref[...])
    o_ref[...] = (z * gate).astype(o_ref.dtype)

    @pl.when(step == pl.num_programs(0) - 1)
    def _():
        sr_ref[...] = st_r[...]
        si_ref[...] = st_i[...]


def s5_mixer(u_tm, u_block, wb, wc, ab, d, wg, bg, x0r, x0i, *, R, T, Tc, W):
    NQ, _, PQ2 = wb.shape
    PQ = PQ2 // 2
    S = NQ * PQ
    rows = Tc * R
    Rp = max(R, SUBLANES)
    LC = min(S, 512)
    in_specs = [
        pl.BlockSpec((rows, W), lambda s: (s, u_block)),
        pl.BlockSpec(wb.shape, lambda s: (0, 0, 0)),
        pl.BlockSpec(wc.shape, lambda s: (0, 0, 0, 0)),
        pl.BlockSpec(ab.shape, lambda s: (0, 0, 0)),
        pl.BlockSpec((1, W), lambda s: (0, 0)),
        pl.BlockSpec((W, W), lambda s: (0, 0)),
        pl.BlockSpec((1, W), lambda s: (0, 0)),
    ]
    args = [u_tm, wb, wc, ab, d, wg, bg]
    if x0r is not None:
        in_specs += [pl.BlockSpec((Rp, S), lambda s: (0, 0))] * 2
        args += [x0r, x0i]
    return pl.pallas_call(
        functools.partial(_s5_body, R=R, Tc=Tc, NQ=NQ, PQ=PQ, LC=LC, has_x0=x0r is not None),
        grid=(T // Tc,),
        in_specs=in_specs,
        out_specs=[pl.BlockSpec((rows, W), lambda s: (s, 0)),
                   pl.BlockSpec((Rp, S), lambda s: (0, 0)),
                   pl.BlockSpec((Rp, S), lambda s: (0, 0))],
        out_shape=[jax.ShapeDtypeStruct((T * R, W), BF16),
                   jax.ShapeDtypeStruct((Rp, S), F32),
                   jax.ShapeDtypeStruct((Rp, S), F32)],
        scratch_shapes=[pltpu.VMEM((rows, S), F32), pltpu.VMEM((rows, S), F32),
                        pltpu.VMEM((rows, W), F32),
                        pltpu.VMEM((Rp, S), F32), pltpu.VMEM((Rp, S), F32)],
        compiler_params=_params("arbitrary"),
        name="s5_mixer",
    )(*args)


def s5_weights(lam_re, lam_im, log_dt, b_re, b_im, c_re, c_im):
    G, P = lam_re.shape
    SG = b_re.shape[-1]
    NQ = G // SSM_SLAB_GROUPS
    dt = jnp.exp(log_dt.astype(F32))[:, None]
    lr, li = lam_re.astype(F32), lam_im.astype(F32)
    mag, ang = jnp.exp(lr * dt), li * dt
    ab_re, ab_im = mag * jnp.cos(ang), mag * jnp.sin(ang)
    den = lr * lr + li * li
    f_re = ((ab_re - 1.0) * lr + ab_im * li) / den
    f_im = (ab_im * lr - (ab_re - 1.0) * li) / den
    br, bi = b_re.astype(F32), b_im.astype(F32)
    bb_re = f_re[..., None] * br - f_im[..., None] * bi
    bb_im = f_re[..., None] * bi + f_im[..., None] * br
    eye = jnp.eye(SSM_SLAB_GROUPS, dtype=F32)

    def slab_in(bb):
        x = bb.reshape(NQ, SSM_SLAB_GROUPS, P, SG)
        return jnp.einsum("qgpi,gh->qgihp", x, eye).reshape(NQ, SSM_SLAB_GROUPS * SG, SSM_SLAB_GROUPS * P)

    def slab_out(cc):
        x = cc.reshape(NQ, SSM_SLAB_GROUPS, SG, P)
        return jnp.einsum("qgop,gh->qgpho", x, eye).reshape(NQ, SSM_SLAB_GROUPS * P, SSM_SLAB_GROUPS * SG)

    wb = jnp.concatenate([slab_in(bb_re), slab_in(bb_im)], axis=-1).astype(BF16)
    wc = jnp.stack([slab_out(c_re.astype(F32)), -slab_out(c_im.astype(F32))]).astype(BF16)
    ab = jnp.stack([jnp.broadcast_to(ab_re.reshape(1, G * P), (SUBLANES, G * P)),
                    jnp.broadcast_to(ab_im.reshape(1, G * P), (SUBLANES, G * P))])
    return wb, wc, ab


def _rope_128(x, cos, sin, half):
    first = lax.broadcasted_iota(jnp.int32, x.shape, 1) < half
    swapped = jnp.where(first, pltpu.roll(x, LANES - half, 1), pltpu.roll(x, half, 1))
    return x * cos + swapped * sin


def _mla_prep_body(cq_ref, ckv_ref, kr_ref, qg_ref, kvg_ref, wq_ref, wuk_ref, cos_ref, sin_ref,
                   q_ref, kcat_ref, ckvo_ref, kro_ref, vt_ref, *, H, NOPE, KV, ROPE, scale):
    cq = cq_ref[...].astype(F32)
    cqn = cq * lax.rsqrt(jnp.mean(cq * cq, axis=-1, keepdims=True) + RMS_EPS) * qg_ref[...]
    qall = jnp.dot(cqn.astype(BF16), wq_ref[...], preferred_element_type=F32)
    cos, sin = cos_ref[...], sin_ref[...]
    for h in range(H):
        qn = qall[:, h * NOPE:(h + 1) * NOPE]
        qr = qall[:, H * NOPE + h * LANES:H * NOPE + (h + 1) * LANES]
        q_lat = jnp.dot(qn.astype(BF16), wuk_ref[h], preferred_element_type=F32)
        q_ref[h, :, 0:KV] = (q_lat * scale).astype(q_ref.dtype)
        q_ref[h, :, KV:KV + LANES] = (_rope_128(qr, cos, sin, ROPE // 2) * scale).astype(q_ref.dtype)
    ckv = ckv_ref[...].astype(F32)
    ckvn = ckv * lax.rsqrt(jnp.mean(ckv * ckv, axis=-1, keepdims=True) + RMS_EPS) * kvg_ref[...]
    krot = _rope_128(kr_ref[...].astype(F32), cos, sin, ROPE // 2)
    ckvo_ref[...] = ckvn
    kro_ref[...] = krot[:, :ROPE]
    kcat_ref[:, 0:KV] = ckvn.astype(kcat_ref.dtype)
    kcat_ref[:, KV:KV + LANES] = krot.astype(kcat_ref.dtype)
    vt_ref[...] = ckvn.T.astype(vt_ref.dtype)


def mla_prep(misc, qg, kvg, wq, wuk, cos, sin, *, H, QL, KV, NOPE, ROPE, bm=256):
    M = misc.shape[0]
    bm = _pick(M, bm)
    P = cos.shape[0]
    if P == M:
        tab_map = lambda i: (i, 0)
    else:
        assert P % bm == 0
        npb = P // bm
        tab_map = lambda i: (i % npb, 0)
    scale = (NOPE + ROPE) ** -0.5
    return pl.pallas_call(
        functools.partial(_mla_prep_body, H=H, NOPE=NOPE, KV=KV, ROPE=ROPE, scale=scale),
        grid=(M // bm,),
        in_specs=[
            pl.BlockSpec((bm, QL), lambda i: (i, 0)),
            pl.BlockSpec((bm, KV), lambda i: (i, QL // KV)),
            pl.BlockSpec((bm, LANES), lambda i: (i, (QL + KV) // LANES)),
            pl.BlockSpec((1, QL), lambda i: (0, 0)),
            pl.BlockSpec((1, KV), lambda i: (0, 0)),
            pl.BlockSpec(wq.shape, lambda i: (0, 0)),
            pl.BlockSpec(wuk.shape, lambda i: (0, 0, 0)),
            pl.BlockSpec((bm, LANES), tab_map),
            pl.BlockSpec((bm, LANES), tab_map),
        ],
        out_specs=[pl.BlockSpec((H, bm, KV + LANES), lambda i: (0, i, 0)),
                   pl.BlockSpec((bm, KV + LANES), lambda i: (i, 0)),
                   pl.BlockSpec((bm, KV), lambda i: (i, 0)),
                   pl.BlockSpec((bm, ROPE), lambda i: (i, 0)),
                   pl.BlockSpec((KV, bm), lambda i: (0, i))],
        out_shape=[jax.ShapeDtypeStruct((H, M, KV + LANES), BF16),
                   jax.ShapeDtypeStruct((M, KV + LANES), BF16),
                   jax.ShapeDtypeStruct((M, KV), F32),
                   jax.ShapeDtypeStruct((M, ROPE), F32),
                   jax.ShapeDtypeStruct((KV, M), BF16)],
        compiler_params=_params("arbitrary"),
        name="mla_prep",
    )(misc, misc, misc, qg.reshape(1, QL), kvg.reshape(1, KV), wq, wuk, cos, sin)


def _attn_prompt_body(qi_tab, kj_tab, q_ref, k_ref, vt_ref, wuvt_ref, o_ref, m_ref, l_ref, acc_ref,
                      *, H, bq, bk, KV, VD):
    p = pl.program_id(1)
    qi, kj = qi_tab[p], kj_tab[p]
    M = H * bq

    @pl.when(kj == 0)
    def _():
        m_ref[...] = jnp.full(m_ref.shape, -jnp.inf, F32)
        l_ref[...] = jnp.zeros(l_ref.shape, F32)
        acc_ref[...] = jnp.zeros(acc_ref.shape, F32)

    def step(masked):
        q = q_ref[...].reshape(M, q_ref.shape[-1])
        st = lax.dot_general(k_ref[...], q, NT_DIMS, preferred_element_type=F32)
        if masked:
            kpos = kj * bk + lax.broadcasted_iota(jnp.int32, (bk, M), 0)
            qpos = qi * bq + lax.broadcasted_iota(jnp.int32, (bk, M), 1) % bq
            st = jnp.where(kpos <= qpos, st, -jnp.inf)
        m_old = m_ref[...]
        m_new = jnp.maximum(m_old, jnp.max(st, axis=0, keepdims=True))
        alpha = jnp.exp(m_old - m_new)
        pt = jnp.exp(st - m_new)
        l_ref[...] = alpha * l_ref[...] + jnp.sum(pt, axis=0, keepdims=True)
        acc_ref[...] = alpha * acc_ref[...] + jnp.dot(vt_ref[...], pt.astype(BF16), preferred_element_type=F32)
        m_ref[...] = m_new

    on_diag = (kj + 1) * bk - 1 > qi * bq

    @pl.when(on_diag)
    def _():
        step(True)

    @pl.when(jnp.logical_not(on_diag))
    def _():
        step(False)

    @pl.when(kj == ((qi + 1) * bq - 1) // bk)
    def _():
        ot = (acc_ref[...] / l_ref[...]).astype(BF16)
        for h in range(H):
            yt = jnp.dot(wuvt_ref[h], ot[:, h * bq:(h + 1) * bq], preferred_element_type=F32)
            o_ref[:, h * VD:(h + 1) * VD] = yt.T.astype(o_ref.dtype)


def mla_attend_prompt(qcat, kcat, vt, wuvt, *, B, T, H, KV, VD, bq=128, bk=512):
    bq, bk = _pick(T, bq), _pick(T, bk)
    nq, nk = T // bq, T // bk
    pairs = [(qi, kj) for qi in range(nq) for kj in range(((qi + 1) * bq - 1) // bk + 1)]
    qi_tab = jnp.asarray(np.array([p[0] for p in pairs], np.int32))
    kj_tab = jnp.asarray(np.array([p[1] for p in pairs], np.int32))
    E = qcat.shape[-1]
    M = H * bq
    grid_spec = pltpu.PrefetchScalarGridSpec(
        num_scalar_prefetch=2,
        grid=(B, len(pairs)),
        in_specs=[
            pl.BlockSpec((H, bq, E), lambda b, p, qt, kt: (0, b * nq + qt[p], 0)),
            pl.BlockSpec((bk, E), lambda b, p, qt, kt: (b * nk + kt[p], 0)),
            pl.BlockSpec((KV, bk), lambda b, p, qt, kt: (0, b * nk + kt[p])),
            pl.BlockSpec(wuvt.shape, lambda b, p, qt, kt: (0, 0, 0)),
        ],
        out_specs=pl.BlockSpec((bq, H * VD), lambda b, p, qt, kt: (b * nq + qt[p], 0)),
        scratch_shapes=[pltpu.VMEM((1, M), F32), pltpu.VMEM((1, M), F32), pltpu.VMEM((KV, M), F32)],
    )
    return pl.pallas_call(
        functools.partial(_attn_prompt_body, H=H, bq=bq, bk=bk, KV=KV, VD=VD),
        grid_spec=grid_spec,
        out_shape=jax.ShapeDtypeStruct((B * T, H * VD), BF16),
        compiler_params=_params("arbitrary", "arbitrary"),
        name="mla_attend_prompt",
    )(qi_tab, kj_tab, qcat, kcat, vt, wuvt)


def _attn_sample_body(pt_ref, q_ref, kn_ref, ck_hbm, kr_hbm, wuv_ref, o_ref,
                      m_ref, l_ref, acc_ref, kbuf, rbuf, sem, *, layer, H, Td, PPS, PAGE, KV, ROPE, VD):
    s_id, j = pl.program_id(0), pl.program_id(1)
    nj = pl.num_programs(1)
    g = s_id * nj + j
    slot = g % 2
    q = q_ref[0]
    M = Td * H
    ql, qr = q[:, :KV], q[:, KV:KV + ROPE]

    def page_copies(seq, step, sl):
        out = []
        for i in range(PPS):
            page = pt_ref[seq, step * PPS + i]
            out.append(pltpu.make_async_copy(ck_hbm.at[layer, page], kbuf.at[sl, i], sem.at[sl]))
            out.append(pltpu.make_async_copy(kr_hbm.at[layer, page], rbuf.at[sl, i], sem.at[sl]))
        return out

    @pl.when(g == 0)
    def _():
        for c in page_copies(0, 0, 0):
            c.start()

    @pl.when(g + 1 < pl.num_programs(0) * nj)
    def _():
        nxt = j + 1
        wrap = nxt == nj
        for c in page_copies(jnp.where(wrap, s_id + 1, s_id), jnp.where(wrap, 0, nxt), 1 - slot):
            c.start()

    pltpu.make_async_copy(ck_hbm.at[layer, pl.ds(0, PPS)], kbuf.at[slot], sem.at[slot]).wait()
    pltpu.make_async_copy(kr_hbm.at[layer, pl.ds(0, PPS)], rbuf.at[slot], sem.at[slot]).wait()

    @pl.when(j == 0)
    def _():
        kn = kn_ref[0]
        TN = kn.shape[0]
        s = lax.dot_general(q, kn, NT_DIMS, preferred_element_type=F32)
        tok = lax.broadcasted_iota(jnp.int32, (M, TN), 0) // H
        col = lax.broadcasted_iota(jnp.int32, (M, TN), 1)
        s = jnp.where(col <= tok, s, -jnp.inf)
        m0 = jnp.max(s, axis=-1, keepdims=True)
        p0 = jnp.exp(s - m0)
        m_ref[...] = m0
        l_ref[...] = jnp.sum(p0, axis=-1, keepdims=True)
        acc_ref[...] = jnp.dot(p0.astype(BF16), kn[:, :KV], preferred_element_type=F32)

    kk = kbuf[slot].reshape(PPS * PAGE, KV).astype(BF16)
    rt = jnp.concatenate([rbuf[slot, i] for i in range(PPS)], axis=1).astype(BF16)
    s = (lax.dot_general(ql, kk, NT_DIMS, preferred_element_type=F32)
         + jnp.dot(qr, rt, preferred_element_type=F32))
    m_old = m_ref[...]
    m_new = jnp.maximum(m_old, jnp.max(s, axis=-1, keepdims=True))
    alpha = jnp.exp(m_old - m_new)
    pr = jnp.exp(s - m_new)
    l_ref[...] = alpha * l_ref[...] + jnp.sum(pr, axis=-1, keepdims=True)
    acc_ref[...] = alpha * acc_ref[...] + jnp.dot(pr.astype(BF16), kk, preferred_element_type=F32)
    m_ref[...] = m_new

    @pl.when(j == pl.num_programs(1) - 1)
    def _():
        o = (acc_ref[...] / l_ref[...]).astype(BF16)
        full = jnp.dot(o, wuv_ref[...], preferred_element_type=F32)
        rh = lax.broadcasted_iota(jnp.int32, (M, H * VD), 0) % H
        ch = lax.broadcasted_iota(jnp.int32, (M, H * VD), 1) // VD
        own = jnp.where(rh == ch, full, 0.0).reshape(Td, H, H * VD)
        o_ref[0] = jnp.sum(own, axis=1).astype(o_ref.dtype)


def mla_attend_sample(page_table, q_s, kn_s, cache_ckv, cache_krT, wuv_all, *, layer, H, Td, KV, ROPE, VD, pps=32):
    Bd, n_pages = page_table.shape
    PAGE = cache_ckv.shape[2]
    pps = _pick(n_pages, pps)
    E = q_s.shape[-1]
    TN = kn_s.shape[1]
    M = Td * H
    grid_spec = pltpu.PrefetchScalarGridSpec(
        num_scalar_prefetch=1,
        grid=(Bd, n_pages // pps),
        in_specs=[pl.BlockSpec((1, M, E), lambda s, j, pt: (s, 0, 0)),
                  pl.BlockSpec((1, TN, E), lambda s, j, pt: (s, 0, 0)),
                  pl.BlockSpec(memory_space=pl.ANY),
                  pl.BlockSpec(memory_space=pl.ANY),
                  pl.BlockSpec(wuv_all.shape, lambda s, j, pt: (0, 0))],
        out_specs=pl.BlockSpec((1, Td, H * VD), lambda s, j, pt: (s, 0, 0)),
        scratch_shapes=[pltpu.VMEM((M, 1), F32), pltpu.VMEM((M, 1), F32), pltpu.VMEM((M, KV), F32),
                        pltpu.VMEM((2, pps, PAGE, KV), F32), pltpu.VMEM((2, pps, ROPE, PAGE), F32),
                        pltpu.SemaphoreType.DMA((2,))],
    )
    return pl.pallas_call(
        functools.partial(_attn_sample_body, layer=layer, H=H, Td=Td, PPS=pps, PAGE=PAGE, KV=KV, ROPE=ROPE, VD=VD),
        grid_spec=grid_spec,
        out_shape=jax.ShapeDtypeStruct((Bd, Td, H * VD), BF16),
        compiler_params=_params("arbitrary", "arbitrary"),
        name="mla_attend_sample",
    )(page_table, q_s, kn_s, cache_ckv, cache_krT, wuv_all)


def _merge_body(x0, x1, x2, w0, w1, w2, g0, g1, g2, o_ref, wb0, wb1, wb2):
    @pl.when(pl.program_id(1) == 0)
    def _():
        wb0[...] = w0[...].astype(BF16)
        wb1[...] = w1[...].astype(BF16)
        wb2[...] = w2[...].astype(BF16)

    acc = g0[...].astype(F32) * jnp.dot(x0[...], wb0[...], preferred_element_type=F32)
    acc = acc + g1[...].astype(F32) * jnp.dot(x1[...], wb1[...], preferred_element_type=F32)
    acc = acc + g2[...].astype(F32) * jnp.dot(x2[...], wb2[...], preferred_element_type=F32)
    o_ref[...] = acc.astype(o_ref.dtype)


def merge_branches(xs, ws, gates, *, layer, bm=512, bn=512):
    M = xs[0].shape[0]
    Ks = [x.shape[1] for x in xs]
    N = ws[0].shape[-1]
    bm, bn = _pick(M, bm), _pick(N, bn)
    nj = N // bn
    in_specs = ([pl.BlockSpec((bm, k), lambda j, i: (i, 0)) for k in Ks]
                + [pl.BlockSpec((None, k, bn), lambda j, i: (layer, 0, j)) for k in Ks]
                + [pl.BlockSpec((bm, bn), (lambda c: (lambda j, i: (i, c * nj + j)))(c)) for c in range(3)])
    return pl.pallas_call(
        _merge_body,
        grid=(nj, M // bm),
        in_specs=in_specs,
        out_specs=pl.BlockSpec((bm, bn), lambda j, i: (i, j)),
        out_shape=jax.ShapeDtypeStruct((M, N), BF16),
        scratch_shapes=[pltpu.VMEM((k, bn), BF16) for k in Ks],
        compiler_params=_params("arbitrary", "arbitrary"),
        name="merge_branches",
    )(*xs, *ws, gates, gates, gates)


def _router_body(h_ref, rw_ref, rb_ref, c0_ref, idx_ref, w_ref, pos_ref, cnt_ref, cnt_s, *, E, NG):
    @pl.when(pl.program_id(0) == 0)
    def _():
        cnt_s[...] = c0_ref[...].astype(F32)

    logits = lax.dot_general(rw_ref[...], h_ref[...].astype(BF16), NT_DIMS, preferred_element_type=F32)
    sc = jax.nn.sigmoid(logits)
    sel = sc + rb_ref[...]
    per = E // NG
    s_rows = [sc[e:e + 1, :] for e in range(E)]
    x_rows = [sel[e:e + 1, :] for e in range(E)]
    gscore = []
    for g in range(NG):
        vals = x_rows[g * per:(g + 1) * per]
        top1 = vals[0]
        top2 = jnp.full_like(top1, -jnp.inf)
        for v in vals[1:]:
            top2 = jnp.maximum(top2, jnp.minimum(top1, v))
            top1 = jnp.maximum(top1, v)
        gscore.append(top1 + top2)
    best = jnp.zeros_like(gscore[0], dtype=jnp.int32)
    bestv = gscore[0]
    for g in range(1, NG):
        upd = gscore[g] > bestv
        best = jnp.where(upd, g, best)
        bestv = jnp.where(upd, gscore[g], bestv)
    masked = [jnp.where(best == (e // per), x_rows[e], -jnp.inf) for e in range(E)]

    def arg_top(vals):
        bv = vals[0]
        bi = jnp.zeros_like(best)
        bs = s_rows[0]
        for e in range(1, E):
            upd = vals[e] > bv
            bv = jnp.where(upd, vals[e], bv)
            bi = jnp.where(upd, e, bi)
            bs = jnp.where(upd, s_rows[e], bs)
        return bi, bs

    i1, s1 = arg_top(masked)
    i2, s2 = arg_top([jnp.where(i1 == e, -jnp.inf, masked[e]) for e in range(E)])
    tot = s1 + s2
    idx_ref[0:1, :] = i1
    idx_ref[1:2, :] = i2
    w_ref[0:1, :] = s1 / tot
    w_ref[1:2, :] = s2 / tot

    bm = i1.shape[1]
    erow = lax.broadcasted_iota(jnp.int32, (E, bm), 0)
    oh0 = jnp.where(erow == i1, 1.0, 0.0)
    oh1 = jnp.where(erow == i2, 1.0, 0.0)
    before = jnp.where(lax.broadcasted_iota(jnp.int32, (bm, bm), 0) < lax.broadcasted_iota(jnp.int32, (bm, bm), 1),
                       1.0, 0.0).astype(BF16)
    pre0 = jnp.dot(oh0.astype(BF16), before, preferred_element_type=F32)
    pre1 = jnp.dot(oh1.astype(BF16), before, preferred_element_type=F32)
    tot0 = jnp.sum(oh0, axis=1, keepdims=True)
    tot1 = jnp.sum(oh1, axis=1, keepdims=True)
    cnt = cnt_s[...]
    pos_ref[0:1, :] = jnp.sum(oh0 * (cnt + pre0), axis=0, keepdims=True).astype(jnp.int32)
    pos_ref[1:2, :] = jnp.sum(oh1 * (cnt + tot0 + pre1), axis=0, keepdims=True).astype(jnp.int32)
    cnt = cnt + tot0 + tot1
    cnt_s[...] = cnt
    cnt_ref[...] = cnt.astype(jnp.int32)


def router(h, rwT, rb, cnt0, *, bm=512):
    M, D = h.shape
    E = rwT.shape[0]
    bm = _pick(M, bm)
    return pl.pallas_call(
        functools.partial(_router_body, E=E, NG=N_EXPERT_GROUPS),
        grid=(M // bm,),
        in_specs=[pl.BlockSpec((bm, D), lambda i: (i, 0)),
                  pl.BlockSpec((E, D), lambda i: (0, 0)),
                  pl.BlockSpec((E, 1), lambda i: (0, 0)),
                  pl.BlockSpec((E, 1), lambda i: (0, 0))],
        out_specs=[pl.BlockSpec((TOP_K, bm), lambda i: (0, i)),
                   pl.BlockSpec((TOP_K, bm), lambda i: (0, i)),
                   pl.BlockSpec((TOP_K, bm), lambda i: (0, i)),
                   pl.BlockSpec((E, 1), lambda i: (0, 0))],
        out_shape=[jax.ShapeDtypeStruct((TOP_K, M), jnp.int32),
                   jax.ShapeDtypeStruct((TOP_K, M), F32),
                   jax.ShapeDtypeStruct((TOP_K, M), jnp.int32),
                   jax.ShapeDtypeStruct((E, 1), jnp.int32)],
        scratch_shapes=[pltpu.VMEM((E, 1), F32)],
        compiler_params=_params("arbitrary"),
        name="moe_router",
    )(h, rwT, rb, cnt0)


def _moe_body(te_ref, nv_ref, x_ref, wgu_ref, wd_ref, y_ref, *, F):
    del te_ref
    i = pl.program_id(0)

    @pl.when(nv_ref[i] > 0)
    def _():
        x = x_ref[...].astype(BF16)
        ab = jnp.dot(x, wgu_ref[...], preferred_element_type=F32)
        t = (_silu(ab[:, :F]) * ab[:, F:]).astype(BF16)
        y_ref[...] = jnp.dot(t, wd_ref[...], preferred_element_type=F32)

    @pl.when(nv_ref[i] == 0)
    def _():
        y_ref[...] = jnp.zeros(y_ref.shape, F32)


def moe_experts(xs, wgu, wd, tile_expert, tile_nvalid, *, layer, bm):
    n_slots, D = xs.shape
    F2 = wgu.shape[-1]
    F = F2 // 2
    grid_spec = pltpu.PrefetchScalarGridSpec(
        num_scalar_prefetch=2,
        grid=(n_slots // bm,),
        in_specs=[
            pl.BlockSpec((bm, D), lambda i, te, nv: (i, 0)),
            pl.BlockSpec((None, None, D, F2), lambda i, te, nv: (layer, te[i], 0, 0)),
            pl.BlockSpec((None, None, F, D), lambda i, te, nv: (layer, te[i], 0, 0)),
        ],
        out_specs=pl.BlockSpec((bm, D), lambda i, te, nv: (i, 0)),
    )
    return pl.pallas_call(
        functools.partial(_moe_body, F=F),
        grid_spec=grid_spec,
        out_shape=jax.ShapeDtypeStruct((n_slots, D), F32),
        compiler_params=_params("arbitrary"),
        name="moe_experts",
    )(tile_expert, tile_nvalid, xs, wgu, wd)


def moe_plan(counts, idx_pos, *, E, bm, n_slots):
    counts = counts.reshape(E)
    padded = ((counts + bm - 1) // bm) * bm
    ends = jnp.cumsum(padded)
    offs = ends - padded
    eids = jnp.arange(E, dtype=jnp.int32)
    slots = [jnp.sum(jnp.where(idx[..., None] == eids, offs, 0), axis=-1) + pos for idx, pos in idx_pos]
    tile_start = jnp.arange(n_slots // bm, dtype=jnp.int32) * bm
    tile_expert = jnp.minimum(jnp.sum((tile_start[:, None] >= ends[None, :]).astype(jnp.int32), axis=1), E - 1)
    first = jnp.sum(jnp.where(tile_expert[:, None] == eids, offs + counts, 0), axis=-1)
    tile_nvalid = jnp.clip(first - tile_start, 0, bm).astype(jnp.int32)
    return tile_expert.astype(jnp.int32), tile_nvalid, slots


def _dispatch_body(slot_ref, h_ref, xs_in, xs_out, sem, *, bt):
    del xs_in

    def row_copy(r, k):
        return pltpu.make_async_copy(h_ref.at[pl.ds(r, 1)], xs_out.at[pl.ds(slot_ref[k, r], 1)], sem)

    def start(r, c):
        for k in range(TOP_K):
            row_copy(r, k).start()
        return c

    lax.fori_loop(0, bt, start, 0, unroll=8)
    for k in range(TOP_K):
        pltpu.make_async_copy(h_ref, xs_out.at[pl.ds(0, bt)], sem).wait()


def moe_dispatch(h, slot, xs, *, bt=256):
    M, D = h.shape
    bt = _pick(M, bt)
    return pl.pallas_call(
        functools.partial(_dispatch_body, bt=bt),
        grid=(M // bt,),
        in_specs=[pl.BlockSpec((TOP_K, bt), lambda i: (0, i), memory_space=pltpu.SMEM),
                  pl.BlockSpec((bt, D), lambda i: (i, 0)),
                  pl.BlockSpec(memory_space=pl.ANY)],
        out_specs=pl.BlockSpec(memory_space=pl.ANY),
        out_shape=jax.ShapeDtypeStruct(xs.shape, xs.dtype),
        scratch_shapes=[pltpu.SemaphoreType.DMA(())],
        input_output_aliases={2: 0},
        compiler_params=_params("arbitrary"),
        name="moe_dispatch",
    )(slot, h, xs)


def _combine_body(*refs, mode, final, bt):
    it = iter(refs)
    slot_ref, x_ref, w_ref, g_ref = next(it), next(it), next(it), next(it)
    ng_ref = next(it) if final else None
    ys_hbm, o_ref, ybuf, sem = next(it), next(it), next(it), next(it)

    def start(r, c):
        for k in range(TOP_K):
            pltpu.make_async_copy(ys_hbm.at[pl.ds(slot_ref[k, r], 1)], ybuf.at[k, pl.ds(r, 1)], sem).start()
        return c

    lax.fori_loop(0, bt, start, 0, unroll=8)
    for k in range(TOP_K):
        pltpu.make_async_copy(ys_hbm.at[pl.ds(0, bt)], ybuf.at[k], sem).wait()
    w = w_ref[...]
    y = w[:, 0:1] * ybuf[0] + w[:, 1:2] * ybuf[1]
    g = g_ref[0] if mode == "group" else g_ref[...]
    x = x_ref[...] + g * y
    if final:
        x = x * lax.rsqrt(jnp.mean(x * x, axis=-1, keepdims=True) + RMS_EPS) * ng_ref[...]
    o_ref[...] = x


def moe_combine(x, ys, slot, w, mod, *, gate_chunk, rows_per_group=None, final_g=None, bt=256):
    M, D = x.shape
    bt = _pick(rows_per_group or M, bt)
    in_specs = [pl.BlockSpec((TOP_K, bt), lambda i: (0, i), memory_space=pltpu.SMEM),
                pl.BlockSpec((bt, D), lambda i: (i, 0)),
                pl.BlockSpec((bt, TOP_K), lambda i: (i, 0))]
    if rows_per_group is not None:
        mode = "group"
        rpg = rows_per_group // bt
        in_specs.append(pl.BlockSpec((1, 1, D), lambda i: (i // rpg, 0, gate_chunk)))
    else:
        mode = "token"
        in_specs.append(pl.BlockSpec((bt, D), lambda i: (i, gate_chunk)))
    args = [slot, x, w, mod]
    if final_g is not None:
        in_specs.append(pl.BlockSpec((1, D), lambda i: (0, 0)))
        args.append(final_g.reshape(1, D))
    in_specs.append(pl.BlockSpec(memory_space=pl.ANY))
    args.append(ys)
    return pl.pallas_call(
        functools.partial(_combine_body, mode=mode, final=final_g is not None, bt=bt),
        grid=(M // bt,),
        in_specs=in_specs,
        out_specs=pl.BlockSpec((bt, D), lambda i: (i, 0)),
        out_shape=jax.ShapeDtypeStruct((M, D), F32),
        scratch_shapes=[pltpu.VMEM((TOP_K, bt, D), F32), pltpu.SemaphoreType.DMA(())],
        compiler_params=_params("arbitrary"),
        name="moe_combine",
    )(*args)


def _rope_tables(pos, rope_dim):
    half = rope_dim // 2
    inv = ROPE_THETA ** (-jnp.arange(half, dtype=F32) / half)
    ang = pos.astype(F32)[:, None] * inv[None, :]
    cos, sin = jnp.cos(ang), jnp.sin(ang)
    pad = jnp.zeros((pos.shape[0], LANES - rope_dim), F32)
    return (jnp.concatenate([cos, cos, pad], axis=1), jnp.concatenate([-sin, sin, pad], axis=1))


def _pad_cols(w, n):
    return jnp.pad(w, ((0, 0), (0, n - w.shape[1])))


def kernel(x_prompt, x_sample, cache_ckv, cache_krope, state_gla, state_ssm_re, state_ssm_im,
           page_table, c_prompt, c_sample, norm_mix, norm_ffn, norm_final, w_ada, b_ada, w_in,
           gla_w_a2, gla_b_a2, gla_norm, ssm_lambda_re, ssm_lambda_im, ssm_log_dt, ssm_b_re,
           ssm_b_im, ssm_c_re, ssm_c_im, ssm_d, ssm_w_glu, ssm_b_glu, mla_q_norm, mla_kv_norm,
           mla_w_uq, mla_w_uk, mla_w_uv, w_br_gla, w_br_ssm, w_br_mla, w_out, router_w,
           router_bias, moe_w_gu, moe_w_down):
    B, T, D = x_prompt.shape
    Bd, Td = x_sample.shape[:2]
    depth = w_in.shape[0]
    _, _, H, DK, DV = state_gla.shape
    QK, VW = H * DK, H * DV
    RANK = gla_w_a2.shape[1]
    G, P = ssm_lambda_re.shape[1:]
    SW = ssm_d.shape[1]
    QL = mla_q_norm.shape[1]
    KV, MH, NOPE = mla_w_uk.shape[1:]
    ROPE = cache_krope.shape[-1]
    VD = mla_w_uv.shape[-1]
    E = router_w.shape[1]
    F = moe_w_down.shape[2]
    n_pages = page_table.shape[1]
    PAGE = cache_ckv.shape[2]
    past_len = n_pages * PAGE
    Np, Ns = B * T, Bd * Td
    n_all = Np + Ns
    Tdp = SUBLANES
    TN = 16

    o_q, o_k, o_v, o_g = 0, QK, 2 * QK, 2 * QK + VW
    o_a = o_g + VW
    o_u = o_a + RANK
    o_cq = o_u + SW
    o_ckv = o_cq + QL
    o_kr = o_ckv + KV
    o_gt = o_kr + ROPE

    cos_p, sin_p = _rope_tables(jnp.arange(T, dtype=jnp.int32), ROPE)
    cos_s, sin_s = _rope_tables(past_len + jnp.arange(Td, dtype=jnp.int32), ROPE)
    cos_s, sin_s = jnp.tile(cos_s, (Bd, 1)), jnp.tile(sin_s, (Bd, 1))

    c_all = jnp.concatenate([c_prompt, c_sample], axis=0)
    rwT = router_w.T.astype(BF16)
    rb = router_bias.astype(F32).reshape(E, 1)
    moe_bm = 256
    n_slots = ((TOP_K * n_all + E * (moe_bm - 1)) // moe_bm) * moe_bm
    wgu_b = moe_w_gu.astype(BF16)
    wd_b = moe_w_down.astype(BF16)
    cache_krT = jnp.swapaxes(cache_krope, 2, 3)

    xp = x_prompt.reshape(Np, D)
    xs = x_sample.reshape(Ns, D)
    outs = {k: [] for k in ("ckv_p", "kr_p", "ckv_s", "kr_s", "gla_p", "gla_s",
                            "sre_p", "sim_p", "sre_s", "sim_s")}
    y_p = y_s = None
    for l in range(depth):
        wl = w_in[l]
        w_qkvg = wl[:, :o_a]
        w_misc = jnp.concatenate([wl[:, o_cq:o_kr], _pad_cols(wl[:, o_kr:o_gt], LANES),
                                  _pad_cols(wl[:, o_a:o_u], LANES)], axis=1)
        misc_w = w_misc.shape[1]
        w_u = wl[:, o_u:o_cq]
        w_gt = wl[:, o_gt:]
        alr_block = (QL + KV + LANES) // LANES
        wa = jnp.pad(gla_w_a2[l], ((0, LANES - RANK), (0, 0))).astype(BF16)
        ba = gla_b_a2[l].astype(F32).reshape(1, QK)
        ng = gla_norm[l].astype(F32).reshape(1, DV)
        wb, wc, ab = s5_weights(ssm_lambda_re[l], ssm_lambda_im[l], ssm_log_dt[l], ssm_b_re[l],
                                ssm_b_im[l], ssm_c_re[l], ssm_c_im[l])
        sd = ssm_d[l].astype(F32).reshape(1, SW)
        wglu = ssm_w_glu[l].astype(BF16)
        bglu = ssm_b_glu[l].astype(F32).reshape(1, SW)
        wq3 = mla_w_uq[l].reshape(QL, MH, NOPE + ROPE)
        wq = jnp.concatenate([wq3[:, :, :NOPE].reshape(QL, MH * NOPE),
                              jnp.pad(wq3[:, :, NOPE:], ((0, 0), (0, 0), (0, LANES - ROPE))).reshape(QL, MH * LANES)],
                             axis=1).astype(BF16)
        wuk = jnp.transpose(mla_w_uk[l], (1, 2, 0)).astype(BF16)
        wuvt = jnp.transpose(mla_w_uv[l], (1, 2, 0)).astype(BF16)
        wuv_all = mla_w_uv[l].reshape(KV, MH * VD).astype(BF16)

        mod = matmul(c_all, w_ada, layer=l, bias=b_ada[l], pre="silu", name="ada_mod")
        mod_p = mod[:B].reshape(B, 1, N_MOD * D)
        mod_s = jnp.repeat(mod[B:], Td, axis=0)

        def mixers(x, *, prompt):
            nb, nt = (B, T) if prompt else (Bd, Td)
            n = nb * nt
            rpg = T if prompt else None
            m = mod_p if prompt else mod_s
            h = norm_mod(x, norm_mix[l], m, sc_chunk=1, sh_chunk=0, rows_per_group=rpg)
            qkvg = matmul(h, w_qkvg, out_dtype=BF16, name="in_qkvg")
            misc = matmul(h, w_misc, out_dtype=F32, name="in_misc")
            u = matmul(h, w_u, out_dtype=F32, name="in_u")
            gates = matmul(h, w_gt, epilogue="sigmoid", out_dtype=BF16, name="in_gates")
            if prompt:
                o_gla, gla_s = gla_mixer(qkvg.reshape(nb, nt, -1), misc.reshape(nb, nt, -1), wa, ba, ng, None,
                                         B=nb, T=nt, H=H, DK=DK, DV=DV, C=_pick(nt, 128), t_valid=_pick(nt, 128),
                                         alr_block=alr_block, Bb=2 if nb % 2 == 0 else 1)
                o_gla = o_gla.reshape(n, VW)
            else:
                padt = ((0, 0), (0, Tdp - nt), (0, 0))
                o_gla, gla_s = gla_mixer(jnp.pad(qkvg.reshape(nb, nt, -1), padt),
                                         jnp.pad(misc.reshape(nb, nt, -1), padt), wa, ba, ng, state_gla[l],
                                         B=nb, T=Tdp, H=H, DK=DK, DV=DV, C=Tdp, t_valid=nt,
                                         alr_block=alr_block, Bb=_pick(nb, 4))
                o_gla = o_gla[:, :nt].reshape(n, VW)
            u_tm = u.reshape(nb, nt, SW).transpose(1, 0, 2).reshape(n, SW)
            if prompt:
                o_ssm, s_re, s_im = s5_mixer(u_tm, 0, wb, wc, ab, sd, wglu, bglu, None, None,
                                             R=nb, T=nt, Tc=_pick(nt, 128), W=SW)
                s_re, s_im = s_re[nb:2 * nb], s_im[nb:2 * nb]
            else:
                o_ssm, s_re, s_im = s5_mixer(u_tm, 0, wb, wc, ab, sd, wglu, bglu,
                                             state_ssm_re[l].reshape(nb, G * P), state_ssm_im[l].reshape(nb, G * P),
                                             R=nb, T=nt, Tc=_pick(nt, 2), W=SW)
            o_ssm = o_ssm.reshape(nt, nb, SW).transpose(1, 0, 2).reshape(n, SW)
            cos, sin = (cos_p, sin_p) if prompt else (cos_s, sin_s)
            qcat, kcat, ckv_n, kr_r, vt = mla_prep(misc, mla_q_norm[l], mla_kv_norm[l], wq, wuk, cos, sin,
                                                   H=MH, QL=QL, KV=KV, NOPE=NOPE, ROPE=ROPE)
            if prompt:
                o_mla = mla_attend_prompt(qcat, kcat, vt, wuvt, B=nb, T=nt, H=MH, KV=KV, VD=VD)
            else:
                q_s = qcat.reshape(MH, nb, nt, -1).transpose(1, 2, 0, 3).reshape(nb, nt * MH, -1)
                kn_s = jnp.pad(kcat.reshape(nb, nt, -1), ((0, 0), (0, TN - nt), (0, 0)))
                o_mla = mla_attend_sample(page_table, q_s, kn_s, cache_ckv, cache_krT, wuv_all, layer=l,
                                          H=MH, Td=nt, KV=KV, ROPE=ROPE, VD=VD).reshape(n, MH * VD)
            merged = merge_branches([o_gla, o_ssm, o_mla], [w_br_gla, w_br_ssm, w_br_mla], gates, layer=l)
            x = matmul(merged, w_out, layer=l, res=x, gate=m, gate_chunk=2, rows_per_group=rpg, name="out_proj")
            hf = norm_mod(x, norm_ffn[l], m, sc_chunk=4, sh_chunk=3, rows_per_group=rpg, out_dtype=F32,
                          name="norm_ffn")
            state = (ckv_n.reshape(nb, nt, KV), kr_r.reshape(nb, nt, ROPE), gla_s,
                     s_re.reshape(nb, G, P), s_im.reshape(nb, G, P))
            return x, hf, state

        xp, hf_p, st_p = mixers(xp, prompt=True)
        xs, hf_s, st_s = mixers(xs, prompt=False)
        for k, v in zip(("ckv_p", "kr_p", "gla_p", "sre_p", "sim_p"), st_p):
            outs[k].append(v)
        for k, v in zip(("ckv_s", "kr_s", "gla_s", "sre_s", "sim_s"), st_s):
            outs[k].append(v)

        idx_p, wts_p, pos_p, cnt = router(hf_p, rwT, rb, jnp.zeros((E, 1), jnp.int32))
        idx_s, wts_s, pos_s, cnt = router(hf_s, rwT, rb, cnt)
        tile_expert, tile_nvalid, (slot_p, slot_s) = moe_plan(
            cnt, [(idx_p, pos_p), (idx_s, pos_s)], E=E, bm=moe_bm, n_slots=n_slots)
        x_slots = moe_dispatch(hf_p, slot_p, jnp.zeros((n_slots, D), F32))
        x_slots = moe_dispatch(hf_s, slot_s, x_slots)
        y_slots = moe_experts(x_slots, wgu_b, wd_b, tile_expert, tile_nvalid, layer=l, bm=moe_bm)
        last = l == depth - 1
        fg = norm_final if last else None
        xp = moe_combine(xp, y_slots, slot_p, wts_p.T, mod_p, gate_chunk=5, rows_per_group=T, final_g=fg)
        xs = moe_combine(xs, y_slots, slot_s, wts_s.T, mod_s, gate_chunk=5, final_g=fg)
        if last:
            y_p, y_s = xp, xs

    st = lambda k: jnp.stack(outs[k])
    return (y_p.reshape(B, T, D), y_s.reshape(Bd, Td, D),
            st("ckv_p"), st("kr_p"), st("ckv_s"), st("kr_s"), st("gla_p"), st("gla_s"),
            st("sre_p"), st("sim_p"), st("sre_s"), st("sim_s"))
```

```python
import functools
import math

import jax
import jax.numpy as jnp
import numpy as np
from jax import lax
from jax.experimental import pallas as pl
from jax.experimental.pallas import tpu as pltpu

F32 = jnp.float32
BF16 = jnp.bfloat16

GLA_TAU = 16.0
ROPE_THETA = 10000.0
RMS_EPS = 1e-6
N_EXPERT_GROUPS = 4
TOP_K = 2
N_MOD = 6
N_BRANCH = 3
SSM_SLAB_GROUPS = 8

VMEM_LIMIT_BYTES = 52 * 1024 * 1024
LANES = 128
SUBLANES = 8

NT_DIMS = (((1,), (1,)), ((), ()))


def _params(*sem):
    return pltpu.CompilerParams(dimension_semantics=sem, vmem_limit_bytes=VMEM_LIMIT_BYTES)


def _pick(n, pref):
    if n <= pref:
        return n
    t = pref
    while n % t:
        t //= 2
    return t


def _silu(x):
    return x * jax.nn.sigmoid(x)


def _mm_body(*refs, pre, epilogue, has_bias, res_mode):
    it = iter(refs)
    x_ref, w_ref = next(it), next(it)
    b_ref = next(it) if has_bias else None
    r_ref = g_ref = None
    if res_mode:
        r_ref, g_ref = next(it), next(it)
    o_ref, wb_ref = next(it), next(it)

    @pl.when(pl.program_id(1) == 0)
    def _():
        wb_ref[...] = w_ref[...].astype(BF16)

    x = x_ref[...]
    if pre == "silu":
        x = _silu(x.astype(F32))
    elif pre == "swiglu":
        f = x.shape[1] // 2
        xf = x.astype(F32)
        x = _silu(xf[:, :f]) * xf[:, f:]
    acc = jnp.dot(x.astype(BF16), wb_ref[...], preferred_element_type=F32)
    if has_bias:
        acc = acc + b_ref[...]
    if epilogue == "sigmoid":
        acc = jax.nn.sigmoid(acc)
    if res_mode == "group":
        acc = r_ref[...] + g_ref[0] * acc
    elif res_mode == "token":
        acc = r_ref[...] + g_ref[...] * acc
    o_ref[...] = acc.astype(o_ref.dtype)


def matmul(x, w, *, bias=None, pre=None, epilogue=None, out_dtype=F32, bm=512, bn=1024,
           res=None, gate=None, gate_chunk=0, rows_per_group=None, layer=None, name="matmul"):
    M, K = x.shape
    N = w.shape[-1]
    bm, bn = _pick(rows_per_group or M, bm), _pick(N, bn)
    nj = N // bn
    kw = K // 2 if pre == "swiglu" else K
    assert w.shape[-2] == kw
    if layer is None:
        w_spec = pl.BlockSpec((kw, bn), lambda j, i: (0, j))
    else:
        w_spec = pl.BlockSpec((None, kw, bn), lambda j, i: (layer, 0, j))
    in_specs = [pl.BlockSpec((bm, K), lambda j, i: (i, 0)), w_spec]
    args = [x, w]
    if bias is not None:
        in_specs.append(pl.BlockSpec((1, bn), lambda j, i: (0, j)))
        args.append(bias.reshape(1, N))
    res_mode = None
    if res is not None:
        in_specs.append(pl.BlockSpec((bm, bn), lambda j, i: (i, j)))
        args.append(res)
        if rows_per_group is not None:
            res_mode = "group"
            assert rows_per_group % bm == 0
            rpg = rows_per_group // bm
            in_specs.append(pl.BlockSpec((1, 1, bn), lambda j, i: (i // rpg, 0, gate_chunk * nj + j)))
        else:
            res_mode = "token"
            in_specs.append(pl.BlockSpec((bm, bn), lambda j, i: (i, gate_chunk * nj + j)))
        args.append(gate)
    return pl.pallas_call(
        functools.partial(_mm_body, pre=pre, epilogue=epilogue, has_bias=bias is not None,
                          res_mode=res_mode),
        grid=(nj, M // bm),
        in_specs=in_specs,
        out_specs=pl.BlockSpec((bm, bn), lambda j, i: (i, j)),
        out_shape=jax.ShapeDtypeStruct((M, N), out_dtype),
        scratch_shapes=[pltpu.VMEM((kw, bn), BF16)],
        compiler_params=_params("arbitrary", "arbitrary"),
        name=name,
    )(*args)


def _norm_body(*refs, mode):
    if mode:
        x_ref, g_ref, sc_ref, sh_ref, o_ref = refs
    else:
        x_ref, g_ref, o_ref = refs
    x = x_ref[...].astype(F32)
    y = x * lax.rsqrt(jnp.mean(x * x, axis=-1, keepdims=True) + RMS_EPS) * g_ref[...]
    if mode == "group":
        y = y * (1.0 + sc_ref[0]) + sh_ref[0]
    elif mode == "token":
        y = y * (1.0 + sc_ref[...]) + sh_ref[...]
    o_ref[...] = y.astype(o_ref.dtype)


def norm_mod(x, g, mod=None, *, sc_chunk=0, sh_chunk=0, rows_per_group=None, out_dtype=BF16,
             bm=512, name="norm_mod"):
    M, D = x.shape
    bm = _pick(rows_per_group or M, bm)
    in_specs = [pl.BlockSpec((bm, D), lambda i: (i, 0)), pl.BlockSpec((1, D), lambda i: (0, 0))]
    args = [x, g.reshape(1, D)]
    mode = None
    if mod is not None:
        if rows_per_group is not None:
            mode = "group"
            rpg = rows_per_group // bm
            in_specs += [pl.BlockSpec((1, 1, D), lambda i: (i // rpg, 0, sc_chunk)),
                         pl.BlockSpec((1, 1, D), lambda i: (i // rpg, 0, sh_chunk))]
        else:
            mode = "token"
            in_specs += [pl.BlockSpec((bm, D), lambda i: (i, sc_chunk)),
                         pl.BlockSpec((bm, D), lambda i: (i, sh_chunk))]
        args += [mod, mod]
    return pl.pallas_call(
        functools.partial(_norm_body, mode=mode),
        grid=(M // bm,),
        in_specs=in_specs,
        out_specs=pl.BlockSpec((bm, D), lambda i: (i, 0)),
        out_shape=jax.ShapeDtypeStruct((M, D), out_dtype),
        compiler_params=_params("arbitrary"),
        name=name,
    )(*args)


def _log_sigmoid(z):
    return jnp.minimum(z, 0.0) - jnp.log1p(jnp.exp(-jnp.abs(z)))


def _gla_body(*refs, H, DK, DV, C, t_valid, has_s0, Bb):
    if has_s0:
        q_ref, k_ref, v_ref, g_ref, a_ref, wa_ref, ba_ref, ng_ref, s0_ref, o_ref, st_ref, sT_ref = refs
    else:
        q_ref, k_ref, v_ref, g_ref, a_ref, wa_ref, ba_ref, ng_ref, o_ref, st_ref, sT_ref = refs
    c = pl.program_id(1)

    @pl.when(c == 0)
    def _():
        for bb in range(Bb):
            for h in range(H):
                if has_s0:
                    sT_ref[bb * H + h] = s0_ref[bb, h].T
                else:
                    sT_ref[bb * H + h] = jnp.zeros((DV, DK), F32)

    row = lax.broadcasted_iota(jnp.int32, (C, 1), 0)
    tri = row >= lax.broadcasted_iota(jnp.int32, (1, C), 1)
    trib = jnp.where(tri, 1.0, 0.0).astype(BF16)
    live = row < t_valid
    mid = C // 2 - 1
    for bb, h in [(bb, h) for bb in range(Bb) for h in range(H)]:
        alr = a_ref[bb].astype(BF16)
        ks = slice(h * DK, (h + 1) * DK)
        vs = slice(h * DV, (h + 1) * DV)
        z = jnp.dot(alr, wa_ref[:, ks], preferred_element_type=F32) + ba_ref[:, ks]
        la = _log_sigmoid(z) * (1.0 / GLA_TAU)
        if t_valid < C:
            la = jnp.where(live, la, 0.0)
        hi = la.astype(BF16)
        lo = (la - hi.astype(F32)).astype(BF16)
        b = (jnp.dot(trib, hi, preferred_element_type=F32)
             + jnp.dot(trib, lo, preferred_element_type=F32))
        b_mid = b[mid:mid + 1, :]
        b_last = b[C - 1:C, :]
        q = q_ref[bb, :, ks].astype(F32) * (DK ** -0.5)
        k = k_ref[bb, :, ks].astype(F32)
        if t_valid < C:
            k = jnp.where(live, k, 0.0)
        v = v_ref[bb, :, vs].astype(BF16)
        qe = (q * jnp.exp(b - b_mid)).astype(BF16)
        ke = (k * jnp.exp(b_mid - b)).astype(BF16)
        a = lax.dot_general(qe, ke, NT_DIMS, preferred_element_type=F32)
        a = jnp.where(tri, a, 0.0)
        qb = (q * jnp.exp(b)).astype(BF16)
        kl = (k * jnp.exp(b_last - b)).astype(BF16)
        sT = sT_ref[bb * H + h]
        o = (jnp.dot(a.astype(BF16), v, preferred_element_type=F32)
             + lax.dot_general(qb, sT.astype(BF16), NT_DIMS, preferred_element_type=F32))
        vT = v_ref[bb, :, vs].astype(F32).T.astype(BF16)
        sT_ref[bb * H + h] = jnp.exp(b_last) * sT + jnp.dot(vT, kl, preferred_element_type=F32)
        on = o * lax.rsqrt(jnp.mean(o * o, axis=-1, keepdims=True) + RMS_EPS) * ng_ref[...]
        gg = g_ref[bb, :, vs].astype(F32)
        o_ref[bb, :, vs] = (on * _silu(gg)).astype(o_ref.dtype)

    @pl.when(c == pl.num_programs(1) - 1)
    def _():
        for bb in range(Bb):
            for h in range(H):
                st_ref[bb, h] = sT_ref[bb * H + h].T


def gla_mixer(qkvg, misc, wa, ba, ng, s0, *, B, T, H, DK, DV, C, t_valid, alr_block, Bb=1):
    QK, VW = H * DK, H * DV
    nc = T // C
    assert B % Bb == 0
    in_specs = [
        pl.BlockSpec((Bb, C, QK), lambda b, c: (b, c, 0)),
        pl.BlockSpec((Bb, C, QK), lambda b, c: (b, c, 1)),
        pl.BlockSpec((Bb, C, VW), lambda b, c: (b, c, (2 * QK) // VW)),
        pl.BlockSpec((Bb, C, VW), lambda b, c: (b, c, (2 * QK) // VW + 1)),
        pl.BlockSpec((Bb, C, LANES), lambda b, c: (b, c, alr_block)),
        pl.BlockSpec((LANES, QK), lambda b, c: (0, 0)),
        pl.BlockSpec((1, QK), lambda b, c: (0, 0)),
        pl.BlockSpec((1, DV), lambda b, c: (0, 0)),
    ]
    args = [qkvg, qkvg, qkvg, qkvg, misc, wa, ba, ng]
    if s0 is not None:
        in_specs.append(pl.BlockSpec((Bb, H, DK, DV), lambda b, c: (b, 0, 0, 0)))
        args.append(s0)
    assert (2 * QK) % VW == 0
    return pl.pallas_call(
        functools.partial(_gla_body, H=H, DK=DK, DV=DV, C=C, t_valid=t_valid, has_s0=s0 is not None, Bb=Bb),
        grid=(B // Bb, nc),
        in_specs=in_specs,
        out_specs=[pl.BlockSpec((Bb, C, VW), lambda b, c: (b, c, 0)),
                   pl.BlockSpec((Bb, H, DK, DV), lambda b, c: (b, 0, 0, 0))],
        out_shape=[jax.ShapeDtypeStruct((B, T, VW), BF16),
                   jax.ShapeDtypeStruct((B, H, DK, DV), F32)],
        scratch_shapes=[pltpu.VMEM((Bb * H, DV, DK), F32)],
        compiler_params=_params("arbitrary", "arbitrary"),
        name="gla_mixer",
    )(*args)


def _gelu_tanh(y):
    return 0.5 * y * (1.0 + jnp.tanh(math.sqrt(2.0 / math.pi) * (y + 0.044715 * (y * y * y))))


def _s5_body(*refs, R, Tc, NQ, PQ, LC, has_x0):
    if has_x0:
        (u_ref, wb_ref, wc_ref, ab_ref, d_ref, wg_ref, bg_ref, x0r_ref, x0i_ref,
         o_ref, sr_ref, si_ref, xr, xi, zs, st_r, st_i) = refs
    else:
        (u_ref, wb_ref, wc_ref, ab_ref, d_ref, wg_ref, bg_ref,
         o_ref, sr_ref, si_ref, xr, xi, zs, st_r, st_i) = refs
    step = pl.program_id(0)
    S = NQ * PQ

    @pl.when(step == 0)
    def _():
        if has_x0:
            st_r[...] = x0r_ref[...]
            st_i[...] = x0i_ref[...]
        else:
            st_r[...] = jnp.zeros(st_r.shape, F32)
            st_i[...] = jnp.zeros(st_i.shape, F32)

    for q in range(NQ):
        uq = u_ref[:, q * LANES:(q + 1) * LANES].astype(BF16)
        r = jnp.dot(uq, wb_ref[q], preferred_element_type=F32)
        xr[:, q * PQ:(q + 1) * PQ] = r[:, :PQ]
        xi[:, q * PQ:(q + 1) * PQ] = r[:, PQ:]

    if R % SUBLANES == 0:
        for lc in range(S // LC):
            sl = slice(lc * LC, (lc + 1) * LC)
            ar = ab_ref[0, 0:1, sl]
            ai = ab_ref[1, 0:1, sl]
            cr, ci = st_r[:, sl], st_i[:, sl]
            for t in range(Tc):
                rs = slice(t * R, (t + 1) * R)
                nr = ar * cr - ai * ci + xr[rs, sl]
                ni = ar * ci + ai * cr + xi[rs, sl]
                xr[rs, sl] = nr
                xi[rs, sl] = ni
                cr, ci = nr, ni
            st_r[:, sl] = cr
            st_i[:, sl] = ci
    else:
        assert R * 2 == SUBLANES and Tc % 2 == 0
        lo_half = lax.broadcasted_iota(jnp.int32, (SUBLANES, LC), 0) < R
        for lc in range(S // LC):
            sl = slice(lc * LC, (lc + 1) * LC)
            ar = ab_ref[0, :, sl]
            ai = ab_ref[1, :, sl]

            def pair(j, carry, sl=sl, ar=ar, ai=ai):
                yr, yi = carry
                off = pl.multiple_of(j * SUBLANES, SUBLANES)
                br = xr[pl.ds(off, SUBLANES), sl]
                bi = xi[pl.ds(off, SUBLANES), sl]
                pr = pltpu.roll(yr, R, 0)
                pi_ = pltpu.roll(yi, R, 0)
                zr = ar * pr - ai * pi_ + br
                zi = ar * pi_ + ai * pr + bi
                qr = pltpu.roll(zr, R, 0)
                qi = pltpu.roll(zi, R, 0)
                wr = ar * qr - ai * qi + br
                wi = ar * qi + ai * qr + bi
                nr = jnp.where(lo_half, zr, wr)
                ni = jnp.where(lo_half, zi, wi)
                xr[pl.ds(off, SUBLANES), sl] = nr
                xi[pl.ds(off, SUBLANES), sl] = ni
                return nr, ni

            cr, ci = lax.fori_loop(0, Tc // 2, pair, (st_r[:, sl], st_i[:, sl]))
            st_r[:, sl] = cr
            st_i[:, sl] = ci

    for q in range(NQ):
        ps = slice(q * PQ, (q + 1) * PQ)
        ls = slice(q * LANES, (q + 1) * LANES)
        y = (jnp.dot(xr[:, ps].astype(BF16), wc_ref[0, q], preferred_element_type=F32)
             + jnp.dot(xi[:, ps].astype(BF16), wc_ref[1, q], preferred_element_type=F32))
        y = y + d_ref[:, ls] * u_ref[:, ls].astype(F32)
        zs[:, ls] = _gelu_tanh(y)
    z = zs[...]
    gate = jax.nn.sigmoid(jnp.dot(z.astype(BF16), wg_ref[...], preferred_element_type=F32) + bg_ref[...])
    o_ref[...] = (z * gate).astype(o_ref.dtype)

    @pl.when(step == pl.num_programs(0) - 1)
    def _():
        sr_ref[...] = st_r[...]
        si_ref[...] = st_i[...]


def s5_mixer(u_tm, u_block, wb, wc, ab, d, wg, bg, x0r, x0i, *, R, T, Tc, W):
    NQ, _, PQ2 = wb.shape
    PQ = PQ2 // 2
    S = NQ * PQ
    rows = Tc * R
    Rp = max(R, SUBLANES)
    LC = min(S, 512)
    in_specs = [
        pl.BlockSpec((rows, W), lambda s: (s, u_block)),
        pl.BlockSpec(wb.shape, lambda s: (0, 0, 0)),
        pl.BlockSpec(wc.shape, lambda s: (0, 0, 0, 0)),
        pl.BlockSpec(ab.shape, lambda s: (0, 0, 0)),
        pl.BlockSpec((1, W), lambda s: (0, 0)),
        pl.BlockSpec((W, W), lambda s: (0, 0)),
        pl.BlockSpec((1, W), lambda s: (0, 0)),
    ]
    args = [u_tm, wb, wc, ab, d, wg, bg]
    if x0r is not None:
        in_specs += [pl.BlockSpec((Rp, S), lambda s: (0, 0))] * 2
        args += [x0r, x0i]
    return pl.pallas_call(
        functools.partial(_s5_body, R=R, Tc=Tc, NQ=NQ, PQ=PQ, LC=LC, has_x0=x0r is not None),
        grid=(T // Tc,),
        in_specs=in_specs,
        out_specs=[pl.BlockSpec((rows, W), lambda s: (s, 0)),
                   pl.BlockSpec((Rp, S), lambda s: (0, 0)),
                   pl.BlockSpec((Rp, S), lambda s: (0, 0))],
        out_shape=[jax.ShapeDtypeStruct((T * R, W), BF16),
                   jax.ShapeDtypeStruct((Rp, S), F32),
                   jax.ShapeDtypeStruct((Rp, S), F32)],
        scratch_shapes=[pltpu.VMEM((rows, S), F32), pltpu.VMEM((rows, S), F32),
                        pltpu.VMEM((rows, W), F32),
                        pltpu.VMEM((Rp, S), F32), pltpu.VMEM((Rp, S), F32)],
        compiler_params=_params("arbitrary"),
        name="s5_mixer",
    )(*args)


def s5_weights(lam_re, lam_im, log_dt, b_re, b_im, c_re, c_im):
    G, P = lam_re.shape
    SG = b_re.shape[-1]
    NQ = G // SSM_SLAB_GROUPS
    dt = jnp.exp(log_dt.astype(F32))[:, None]
    lr, li = lam_re.astype(F32), lam_im.astype(F32)
    mag, ang = jnp.exp(lr * dt), li * dt
    ab_re, ab_im = mag * jnp.cos(ang), mag * jnp.sin(ang)
    den = lr * lr + li * li
    f_re = ((ab_re - 1.0) * lr + ab_im * li) / den
    f_im = (ab_im * lr - (ab_re - 1.0) * li) / den
    br, bi = b_re.astype(F32), b_im.astype(F32)
    bb_re = f_re[..., None] * br - f_im[..., None] * bi
    bb_im = f_re[..., None] * bi + f_im[..., None] * br
    eye = jnp.eye(SSM_SLAB_GROUPS, dtype=F32)

    def slab_in(bb):
        x = bb.reshape(NQ, SSM_SLAB_GROUPS, P, SG)
        return jnp.einsum("qgpi,gh->qgihp", x, eye).reshape(NQ, SSM_SLAB_GROUPS * SG, SSM_SLAB_GROUPS * P)

    def slab_out(cc):
        x = cc.reshape(NQ, SSM_SLAB_GROUPS, SG, P)
        return jnp.einsum("qgop,gh->qgpho", x, eye).reshape(NQ, SSM_SLAB_GROUPS * P, SSM_SLAB_GROUPS * SG)

    wb = jnp.concatenate([slab_in(bb_re), slab_in(bb_im)], axis=-1).astype(BF16)
    wc = jnp.stack([slab_out(c_re.astype(F32)), -slab_out(c_im.astype(F32))]).astype(BF16)
    ab = jnp.stack([jnp.broadcast_to(ab_re.reshape(1, G * P), (SUBLANES, G * P)),
                    jnp.broadcast_to(ab_im.reshape(1, G * P), (SUBLANES, G * P))])
    return wb, wc, ab


def _rope_128(x, cos, sin, half):
    first = lax.broadcasted_iota(jnp.int32, x.shape, 1) < half
    swapped = jnp.where(first, pltpu.roll(x, LANES - half, 1), pltpu.roll(x, half, 1))
    return x * cos + swapped * sin


def _mla_prep_body(cq_ref, ckv_ref, kr_ref, qg_ref, kvg_ref, wq_ref, wuk_ref, cos_ref, sin_ref,
                   q_ref, kcat_ref, ckvo_ref, kro_ref, vt_ref, *, H, NOPE, KV, ROPE, scale):
    cq = cq_ref[...].astype(F32)
    cqn = cq * lax.rsqrt(jnp.mean(cq * cq, axis=-1, keepdims=True) + RMS_EPS) * qg_ref[...]
    qall = jnp.dot(cqn.astype(BF16), wq_ref[...], preferred_element_type=F32)
    cos, sin = cos_ref[...], sin_ref[...]
    for h in range(H):
        qn = qall[:, h * NOPE:(h + 1) * NOPE]
        qr = qall[:, H * NOPE + h * LANES:H * NOPE + (h + 1) * LANES]
        q_lat = jnp.dot(qn.astype(BF16), wuk_ref[h], preferred_element_type=F32)
        q_ref[h, :, 0:KV] = (q_lat * scale).astype(q_ref.dtype)
        q_ref[h, :, KV:KV + LANES] = (_rope_128(qr, cos, sin, ROPE // 2) * scale).astype(q_ref.dtype)
    ckv = ckv_ref[...].astype(F32)
    ckvn = ckv * lax.rsqrt(jnp.mean(ckv * ckv, axis=-1, keepdims=True) + RMS_EPS) * kvg_ref[...]
    krot = _rope_128(kr_ref[...].astype(F32), cos, sin, ROPE // 2)
    ckvo_ref[...] = ckvn
    kro_ref[...] = krot[:, :ROPE]
    kcat_ref[:, 0:KV] = ckvn.astype(kcat_ref.dtype)
    kcat_ref[:, KV:KV + LANES] = krot.astype(kcat_ref.dtype)
    vt_ref[...] = ckvn.T.astype(vt_ref.dtype)


def mla_prep(misc, qg, kvg, wq, wuk, cos, sin, *, H, QL, KV, NOPE, ROPE, bm=256):
    M = misc.shape[0]
    bm = _pick(M, bm)
    P = cos.shape[0]
    if P == M:
        tab_map = lambda i: (i, 0)
    else:
        assert P % bm == 0
        npb = P // bm
        tab_map = lambda i: (i % npb, 0)
    scale = (NOPE + ROPE) ** -0.5
    return pl.pallas_call(
        functools.partial(_mla_prep_body, H=H, NOPE=NOPE, KV=KV, ROPE=ROPE, scale=scale),
        grid=(M // bm,),
        in_specs=[
            pl.BlockSpec((bm, QL), lambda i: (i, 0)),
            pl.BlockSpec((bm, KV), lambda i: (i, QL // KV)),
            pl.BlockSpec((bm, LANES), lambda i: (i, (QL + KV) // LANES)),
            pl.BlockSpec((1, QL), lambda i: (0, 0)),
            pl.BlockSpec((1, KV), lambda i: (0, 0)),
            pl.BlockSpec(wq.shape, lambda i: (0, 0)),
            pl.BlockSpec(wuk.shape, lambda i: (0, 0, 0)),
            pl.BlockSpec((bm, LANES), tab_map),
            pl.BlockSpec((bm, LANES), tab_map),
        ],
        out_specs=[pl.BlockSpec((H, bm, KV + LANES), lambda i: (0, i, 0)),
                   pl.BlockSpec((bm, KV + LANES), lambda i: (i, 0)),
                   pl.BlockSpec((bm, KV), lambda i: (i, 0)),
                   pl.BlockSpec((bm, ROPE), lambda i: (i, 0)),
                   pl.BlockSpec((KV, bm), lambda i: (0, i))],
        out_shape=[jax.ShapeDtypeStruct((H, M, KV + LANES), BF16),
                   jax.ShapeDtypeStruct((M, KV + LANES), BF16),
                   jax.ShapeDtypeStruct((M, KV), F32),
                   jax.ShapeDtypeStruct((M, ROPE), F32),
                   jax.ShapeDtypeStruct((KV, M), BF16)],
        compiler_params=_params("arbitrary"),
        name="mla_prep",
    )(misc, misc, misc, qg.reshape(1, QL), kvg.reshape(1, KV), wq, wuk, cos, sin)


def _attn_prompt_body(qi_tab, kj_tab, q_ref, k_ref, vt_ref, wuvt_ref, o_ref, m_ref, l_ref, acc_ref,
                      *, H, bq, bk, KV, VD):
    p = pl.program_id(1)
    qi, kj = qi_tab[p], kj_tab[p]
    M = H * bq

    @pl.when(kj == 0)
    def _():
        m_ref[...] = jnp.full(m_ref.shape, -jnp.inf, F32)
        l_ref[...] = jnp.zeros(l_ref.shape, F32)
        acc_ref[...] = jnp.zeros(acc_ref.shape, F32)

    def step(masked):
        q = q_ref[...].reshape(M, q_ref.shape[-1])
        st = lax.dot_general(k_ref[...], q, NT_DIMS, preferred_element_type=F32)
        if masked:
            kpos = kj * bk + lax.broadcasted_iota(jnp.int32, (bk, M), 0)
            qpos = qi * bq + lax.broadcasted_iota(jnp.int32, (bk, M), 1) % bq
            st = jnp.where(kpos <= qpos, st, -jnp.inf)
        m_old = m_ref[...]
        m_new = jnp.maximum(m_old, jnp.max(st, axis=0, keepdims=True))
        alpha = jnp.exp(m_old - m_new)
        pt = jnp.exp(st - m_new)
        l_ref[...] = alpha * l_ref[...] + jnp.sum(pt, axis=0, keepdims=True)
        acc_ref[...] = alpha * acc_ref[...] + jnp.dot(vt_ref[...], pt.astype(BF16), preferred_element_type=F32)
        m_ref[...] = m_new

    on_diag = (kj + 1) * bk - 1 > qi * bq

    @pl.when(on_diag)
    def _():
        step(True)

    @pl.when(jnp.logical_not(on_diag))
    def _():
        step(False)

    @pl.when(kj == ((qi + 1) * bq - 1) // bk)
    def _():
        ot = (acc_ref[...] / l_ref[...]).astype(BF16)
        for h in range(H):
            yt = jnp.dot(wuvt_ref[h], ot[:, h * bq:(h + 1) * bq], preferred_element_type=F32)
            o_ref[:, h * VD:(h + 1) * VD] = yt.T.astype(o_ref.dtype)


def mla_attend_prompt(qcat, kcat, vt, wuvt, *, B, T, H, KV, VD, bq=128, bk=512):
    bq, bk = _pick(T, bq), _pick(T, bk)
    nq, nk = T // bq, T // bk
    pairs = [(qi, kj) for qi in range(nq) for kj in range(((qi + 1) * bq - 1) // bk + 1)]
    qi_tab = jnp.asarray(np.array([p[0] for p in pairs], np.int32))
    kj_tab = jnp.asarray(np.array([p[1] for p in pairs], np.int32))
    E = qcat.shape[-1]
    M = H * bq
    grid_spec = pltpu.PrefetchScalarGridSpec(
        num_scalar_prefetch=2,
        grid=(B, len(pairs)),
        in_specs=[
            pl.BlockSpec((H, bq, E), lambda b, p, qt, kt: (0, b * nq + qt[p], 0)),
            pl.BlockSpec((bk, E), lambda b, p, qt, kt: (b * nk + kt[p], 0)),
            pl.BlockSpec((KV, bk), lambda b, p, qt, kt: (0, b * nk + kt[p])),
            pl.BlockSpec(wuvt.shape, lambda b, p, qt, kt: (0, 0, 0)),
        ],
        out_specs=pl.BlockSpec((bq, H * VD), lambda b, p, qt, kt: (b * nq + qt[p], 0)),
        scratch_shapes=[pltpu.VMEM((1, M), F32), pltpu.VMEM((1, M), F32), pltpu.VMEM((KV, M), F32)],
    )
    return pl.pallas_call(
        functools.partial(_attn_prompt_body, H=H, bq=bq, bk=bk, KV=KV, VD=VD),
        grid_spec=grid_spec,
        out_shape=jax.ShapeDtypeStruct((B * T, H * VD), BF16),
        compiler_params=_params("arbitrary", "arbitrary"),
        name="mla_attend_prompt",
    )(qi_tab, kj_tab, qcat, kcat, vt, wuvt)


def _attn_sample_body(pt_ref, q_ref, kn_ref, ck_hbm, kr_hbm, wuv_ref, o_ref,
                      m_ref, l_ref, acc_ref, kbuf, rbuf, sem, *, layer, H, Td, PPS, PAGE, KV, ROPE, VD):
    s_id, j = pl.program_id(0), pl.program_id(1)
    nj = pl.num_programs(1)
    g = s_id * nj + j
    slot = g % 2
    q = q_ref[0]
    M = Td * H
    ql, qr = q[:, :KV], q[:, KV:KV + ROPE]

    def page_copies(seq, step, sl):
        out = []
        for i in range(PPS):
            page = pt_ref[seq, step * PPS + i]
            out.append(pltpu.make_async_copy(ck_hbm.at[layer, page], kbuf.at[sl, i], sem.at[sl]))
            out.append(pltpu.make_async_copy(kr_hbm.at[layer, page], rbuf.at[sl, i], sem.at[sl]))
        return out

    @pl.when(g == 0)
    def _():
        for c in page_copies(0, 0, 0):
            c.start()

    @pl.when(g + 1 < pl.num_programs(0) * nj)
    def _():
        nxt = j + 1
        wrap = nxt == nj
        for c in page_copies(jnp.where(wrap, s_id + 1, s_id), jnp.where(wrap, 0, nxt), 1 - slot):
            c.start()

    pltpu.make_async_copy(ck_hbm.at[layer, pl.ds(0, PPS)], kbuf.at[slot], sem.at[slot]).wait()
    pltpu.make_async_copy(kr_hbm.at[layer, pl.ds(0, PPS)], rbuf.at[slot], sem.at[slot]).wait()

    @pl.when(j == 0)
    def _():
        kn = kn_ref[0]
        TN = kn.shape[0]
        s = lax.dot_general(q, kn, NT_DIMS, preferred_element_type=F32)
        tok = lax.broadcasted_iota(jnp.int32, (M, TN), 0) // H
        col = lax.broadcasted_iota(jnp.int32, (M, TN), 1)
        s = jnp.where(col <= tok, s, -jnp.inf)
        m0 = jnp.max(s, axis=-1, keepdims=True)
        p0 = jnp.exp(s - m0)
        m_ref[...] = m0
        l_ref[...] = jnp.sum(p0, axis=-1, keepdims=True)
        acc_ref[...] = jnp.dot(p0.astype(BF16), kn[:, :KV], preferred_element_type=F32)

    kk = kbuf[slot].reshape(PPS * PAGE, KV).astype(BF16)
    rt = jnp.concatenate([rbuf[slot, i] for i in range(PPS)], axis=1).astype(BF16)
    s = (lax.dot_general(ql, kk, NT_DIMS, preferred_element_type=F32)
         + jnp.dot(qr, rt, preferred_element_type=F32))
    m_old = m_ref[...]
    m_new = jnp.maximum(m_old, jnp.max(s, axis=-1, keepdims=True))
    alpha = jnp.exp(m_old - m_new)
    pr = jnp.exp(s - m_new)
    l_ref[...] = alpha * l_ref[...] + jnp.sum(pr, axis=-1, keepdims=True)
    acc_ref[...] = alpha * acc_ref[...] + jnp.dot(pr.astype(BF16), kk, preferred_element_type=F32)
    m_ref[...] = m_new

    @pl.when(j == pl.num_programs(1) - 1)
    def _():
        o = (acc_ref[...] / l_ref[...]).astype(BF16)
        full = jnp.dot(o, wuv_ref[...], preferred_element_type=F32)
        rh = lax.broadcasted_iota(jnp.int32, (M, H * VD), 0) % H
        ch = lax.broadcasted_iota(jnp.int32, (M, H * VD), 1) // VD
        own = jnp.where(rh == ch, full, 0.0).reshape(Td, H, H * VD)
        o_ref[0] = jnp.sum(own, axis=1).astype(o_ref.dtype)


def mla_attend_sample(page_table, q_s, kn_s, cache_ckv, cache_krT, wuv_all, *, layer, H, Td, KV, ROPE, VD, pps=32):
    Bd, n_pages = page_table.shape
    PAGE = cache_ckv.shape[2]
    pps = _pick(n_pages, pps)
    E = q_s.shape[-1]
    TN = kn_s.shape[1]
    M = Td * H
    grid_spec = pltpu.PrefetchScalarGridSpec(
        num_scalar_prefetch=1,
        grid=(Bd, n_pages // pps),
        in_specs=[pl.BlockSpec((1, M, E), lambda s, j, pt: (s, 0, 0)),
                  pl.BlockSpec((1, TN, E), lambda s, j, pt: (s, 0, 0)),
                  pl.BlockSpec(memory_space=pl.ANY),
                  pl.BlockSpec(memory_space=pl.ANY),
                  pl.BlockSpec(wuv_all.shape, lambda s, j, pt: (0, 0))],
        out_specs=pl.BlockSpec((1, Td, H * VD), lambda s, j, pt: (s, 0, 0)),
        scratch_shapes=[pltpu.VMEM((M, 1), F32), pltpu.VMEM((M, 1), F32), pltpu.VMEM((M, KV), F32),
                        pltpu.VMEM((2, pps, PAGE, KV), F32), pltpu.VMEM((2, pps, ROPE, PAGE), F32),
                        pltpu.SemaphoreType.DMA((2,))],
    )
    return pl.pallas_call(
        functools.partial(_attn_sample_body, layer=layer, H=H, Td=Td, PPS=pps, PAGE=PAGE, KV=KV, ROPE=ROPE, VD=VD),
        grid_spec=grid_spec,
        out_shape=jax.ShapeDtypeStruct((Bd, Td, H * VD), BF16),
        compiler_params=_params("arbitrary", "arbitrary"),
        name="mla_attend_sample",
    )(page_table, q_s, kn_s, cache_ckv, cache_krT, wuv_all)


def _merge_body(x0, x1, x2, w0, w1, w2, g0, g1, g2, o_ref, wb0, wb1, wb2):
    @pl.when(pl.program_id(1) == 0)
    def _():
        wb0[...] = w0[...].astype(BF16)
        wb1[...] = w1[...].astype(BF16)
        wb2[...] = w2[...].astype(BF16)

    acc = g0[...].astype(F32) * jnp.dot(x0[...], wb0[...], preferred_element_type=F32)
    acc = acc + g1[...].astype(F32) * jnp.dot(x1[...], wb1[...], preferred_element_type=F32)
    acc = acc + g2[...].astype(F32) * jnp.dot(x2[...], wb2[...], preferred_element_type=F32)
    o_ref[...] = acc.astype(o_ref.dtype)


def merge_branches(xs, ws, gates, *, layer, bm=512, bn=512):
    M = xs[0].shape[0]
    Ks = [x.shape[1] for x in xs]
    N = ws[0].shape[-1]
    bm, bn = _pick(M, bm), _pick(N, bn)
    nj = N // bn
    in_specs = ([pl.BlockSpec((bm, k), lambda j, i: (i, 0)) for k in Ks]
                + [pl.BlockSpec((None, k, bn), lambda j, i: (layer, 0, j)) for k in Ks]
                + [pl.BlockSpec((bm, bn), (lambda c: (lambda j, i: (i, c * nj + j)))(c)) for c in range(3)])
    return pl.pallas_call(
        _merge_body,
        grid=(nj, M // bm),
        in_specs=in_specs,
        out_specs=pl.BlockSpec((bm, bn), lambda j, i: (i, j)),
        out_shape=jax.ShapeDtypeStruct((M, N), BF16),
        scratch_shapes=[pltpu.VMEM((k, bn), BF16) for k in Ks],
        compiler_params=_params("arbitrary", "arbitrary"),
        name="merge_branches",
    )(*xs, *ws, gates, gates, gates)


def _router_body(h_ref, rw_ref, rb_ref, c0_ref, idx_ref, w_ref, pos_ref, cnt_ref, cnt_s, *, E, NG):
    @pl.when(pl.program_id(0) == 0)
    def _():
        cnt_s[...] = c0_ref[...].astype(F32)

    logits = lax.dot_general(rw_ref[...], h_ref[...].astype(BF16), NT_DIMS, preferred_element_type=F32)
    sc = jax.nn.sigmoid(logits)
    sel = sc + rb_ref[...]
    per = E // NG
    s_rows = [sc[e:e + 1, :] for e in range(E)]
    x_rows = [sel[e:e + 1, :] for e in range(E)]
    gscore = []
    for g in range(NG):
        vals = x_rows[g * per:(g + 1) * per]
        top1 = vals[0]
        top2 = jnp.full_like(top1, -jnp.inf)
        for v in vals[1:]:
            top2 = jnp.maximum(top2, jnp.minimum(top1, v))
            top1 = jnp.maximum(top1, v)
        gscore.append(top1 + top2)
    best = jnp.zeros_like(gscore[0], dtype=jnp.int32)
    bestv = gscore[0]
    for g in range(1, NG):
        upd = gscore[g] > bestv
        best = jnp.where(upd, g, best)
        bestv = jnp.where(upd, gscore[g], bestv)
    masked = [jnp.where(best == (e // per), x_rows[e], -jnp.inf) for e in range(E)]

    def arg_top(vals):
        bv = vals[0]
        bi = jnp.zeros_like(best)
        bs = s_rows[0]
        for e in range(1, E):
            upd = vals[e] > bv
            bv = jnp.where(upd, vals[e], bv)
            bi = jnp.where(upd, e, bi)
            bs = jnp.where(upd, s_rows[e], bs)
        return bi, bs

    i1, s1 = arg_top(masked)
    i2, s2 = arg_top([jnp.where(i1 == e, -jnp.inf, masked[e]) for e in range(E)])
    tot = s1 + s2
    idx_ref[0:1, :] = i1
    idx_ref[1:2, :] = i2
    w_ref[0:1, :] = s1 / tot
    w_ref[1:2, :] = s2 / tot

    bm = i1.shape[1]
    erow = lax.broadcasted_iota(jnp.int32, (E, bm), 0)
    oh0 = jnp.where(erow == i1, 1.0, 0.0)
    oh1 = jnp.where(erow == i2, 1.0, 0.0)
    before = jnp.where(lax.broadcasted_iota(jnp.int32, (bm, bm), 0) < lax.broadcasted_iota(jnp.int32, (bm, bm), 1),
                       1.0, 0.0).astype(BF16)
    pre0 = jnp.dot(oh0.astype(BF16), before, preferred_element_type=F32)
    pre1 = jnp.dot(oh1.astype(BF16), before, preferred_element_type=F32)
    tot0 = jnp.sum(oh0, axis=1, keepdims=True)
    tot1 = jnp.sum(oh1, axis=1, keepdims=True)
    cnt = cnt_s[...]
    pos_ref[0:1, :] = jnp.sum(oh0 * (cnt + pre0), axis=0, keepdims=True).astype(jnp.int32)
    pos_ref[1:2, :] = jnp.sum(oh1 * (cnt + tot0 + pre1), axis=0, keepdims=True).astype(jnp.int32)
    cnt = cnt + tot0 + tot1
    cnt_s[...] = cnt
    cnt_ref[...] = cnt.astype(jnp.int32)


def router(h, rwT, rb, cnt0, *, bm=512):
    M, D = h.shape
    E = rwT.shape[0]
    bm = _pick(M, bm)
    return pl.pallas_call(
        functools.partial(_router_body, E=E, NG=N_EXPERT_GROUPS),
        grid=(M // bm,),
        in_specs=[pl.BlockSpec((bm, D), lambda i: (i, 0)),
                  pl.BlockSpec((E, D), lambda i: (0, 0)),
                  pl.BlockSpec((E, 1), lambda i: (0, 0)),
                  pl.BlockSpec((E, 1), lambda i: (0, 0))],
        out_specs=[pl.BlockSpec((TOP_K, bm), lambda i: (0, i)),
                   pl.BlockSpec((TOP_K, bm), lambda i: (0, i)),
                   pl.BlockSpec((TOP_K, bm), lambda i: (0, i)),
                   pl.BlockSpec((E, 1), lambda i: (0, 0))],
        out_shape=[jax.ShapeDtypeStruct((TOP_K, M), jnp.int32),
                   jax.ShapeDtypeStruct((TOP_K, M), F32),
                   jax.ShapeDtypeStruct((TOP_K, M), jnp.int32),
                   jax.ShapeDtypeStruct((E, 1), jnp.int32)],
        scratch_shapes=[pltpu.VMEM((E, 1), F32)],
        compiler_params=_params("arbitrary"),
        name="moe_router",
    )(h, rwT, rb, cnt0)


def _moe_body(te_ref, nv_ref, x_ref, wgu_ref, wd_ref, y_ref, *, F):
    del te_ref
    i = pl.program_id(0)

    @pl.when(nv_ref[i] > 0)
    def _():
        x = x_ref[...].astype(BF16)
        ab = jnp.dot(x, wgu_ref[...], preferred_element_type=F32)
        t = (_silu(ab[:, :F]) * ab[:, F:]).astype(BF16)
        y_ref[...] = jnp.dot(t, wd_ref[...], preferred_element_type=F32)

    @pl.when(nv_ref[i] == 0)
    def _():
        y_ref[...] = jnp.zeros(y_ref.shape, F32)


def moe_experts(xs, wgu, wd, tile_expert, tile_nvalid, *, layer, bm):
    n_slots, D = xs.shape
    F2 = wgu.shape[-1]
    F = F2 // 2
    grid_spec = pltpu.PrefetchScalarGridSpec(
        num_scalar_prefetch=2,
        grid=(n_slots // bm,),
        in_specs=[
            pl.BlockSpec((bm, D), lambda i, te, nv: (i, 0)),
            pl.BlockSpec((None, None, D, F2), lambda i, te, nv: (layer, te[i], 0, 0)),
            pl.BlockSpec((None, None, F, D), lambda i, te, nv: (layer, te[i], 0, 0)),
        ],
        out_specs=pl.BlockSpec((bm, D), lambda i, te, nv: (i, 0)),
    )
    return pl.pallas_call(
        functools.partial(_moe_body, F=F),
        grid_spec=grid_spec,
        out_shape=jax.ShapeDtypeStruct((n_slots, D), F32),
        compiler_params=_params("arbitrary"),
        name="moe_experts",
    )(tile_expert, tile_nvalid, xs, wgu, wd)


def moe_plan(counts, idx_pos, *, E, bm, n_slots):
    counts = counts.reshape(E)
    padded = ((counts + bm - 1) // bm) * bm
    ends = jnp.cumsum(padded)
    offs = ends - padded
    eids = jnp.arange(E, dtype=jnp.int32)
    slots = [jnp.sum(jnp.where(idx[..., None] == eids, offs, 0), axis=-1) + pos for idx, pos in idx_pos]
    tile_start = jnp.arange(n_slots // bm, dtype=jnp.int32) * bm
    tile_expert = jnp.minimum(jnp.sum((tile_start[:, None] >= ends[None, :]).astype(jnp.int32), axis=1), E - 1)
    first = jnp.sum(jnp.where(tile_expert[:, None] == eids, offs + counts, 0), axis=-1)
    tile_nvalid = jnp.clip(first - tile_start, 0, bm).astype(jnp.int32)
    return tile_expert.astype(jnp.int32), tile_nvalid, slots


def _dispatch_body(slot_ref, h_ref, xs_in, xs_out, sem, *, bt):
    del xs_in

    def row_copy(r, k):
        return pltpu.make_async_copy(h_ref.at[pl.ds(r, 1)], xs_out.at[pl.ds(slot_ref[k, r], 1)], sem)

    def start(r, c):
        for k in range(TOP_K):
            row_copy(r, k).start(priority=k % 2)
        return c

    lax.fori_loop(0, bt, start, 0, unroll=8)
    for k in range(TOP_K):
        pltpu.make_async_copy(h_ref, xs_out.at[pl.ds(0, bt)], sem).wait()


def moe_dispatch(h, slot, xs, *, bt=256):
    M, D = h.shape
    bt = _pick(M, bt)
    return pl.pallas_call(
        functools.partial(_dispatch_body, bt=bt),
        grid=(M // bt,),
        in_specs=[pl.BlockSpec((TOP_K, bt), lambda i: (0, i), memory_space=pltpu.SMEM),
                  pl.BlockSpec((bt, D), lambda i: (i, 0)),
                  pl.BlockSpec(memory_space=pl.ANY)],
        out_specs=pl.BlockSpec(memory_space=pl.ANY),
        out_shape=jax.ShapeDtypeStruct(xs.shape, xs.dtype),
        scratch_shapes=[pltpu.SemaphoreType.DMA(())],
        input_output_aliases={2: 0},
        compiler_params=_params("arbitrary"),
        name="moe_dispatch",
    )(slot, h, xs)


def _combine_body(*refs, mode, final, bt):
    it = iter(refs)
    slot_ref, x_ref, w_ref, g_ref = next(it), next(it), next(it), next(it)
    ng_ref = next(it) if final else None
    ys_hbm, o_ref, ybuf, sem = next(it), next(it), next(it), next(it)

    def start(r, c):
        for k in range(TOP_K):
            pltpu.make_async_copy(ys_hbm.at[pl.ds(slot_ref[k, r], 1)], ybuf.at[k, pl.ds(r, 1)],
                                  sem).start(priority=k % 2)
        return c

    lax.fori_loop(0, bt, start, 0, unroll=8)
    for k in range(TOP_K):
        pltpu.make_async_copy(ys_hbm.at[pl.ds(0, bt)], ybuf.at[k], sem).wait()
    w = w_ref[...]
    y = w[:, 0:1] * ybuf[0] + w[:, 1:2] * ybuf[1]
    g = g_ref[0] if mode == "group" else g_ref[...]
    x = x_ref[...] + g * y
    if final:
        x = x * lax.rsqrt(jnp.mean(x * x, axis=-1, keepdims=True) + RMS_EPS) * ng_ref[...]
    o_ref[...] = x


def moe_combine(x, ys, slot, w, mod, *, gate_chunk, rows_per_group=None, final_g=None, bt=256):
    M, D = x.shape
    bt = _pick(rows_per_group or M, bt)
    in_specs = [pl.BlockSpec((TOP_K, bt), lambda i: (0, i), memory_space=pltpu.SMEM),
                pl.BlockSpec((bt, D), lambda i: (i, 0)),
                pl.BlockSpec((bt, TOP_K), lambda i: (i, 0))]
    if rows_per_group is not None:
        mode = "group"
        rpg = rows_per_group // bt
        in_specs.append(pl.BlockSpec((1, 1, D), lambda i: (i // rpg, 0, gate_chunk)))
    else:
        mode = "token"
        in_specs.append(pl.BlockSpec((bt, D), lambda i: (i, gate_chunk)))
    args = [slot, x, w, mod]
    if final_g is not None:
        in_specs.append(pl.BlockSpec((1, D), lambda i: (0, 0)))
        args.append(final_g.reshape(1, D))
    in_specs.append(pl.BlockSpec(memory_space=pl.ANY))
    args.append(ys)
    return pl.pallas_call(
        functools.partial(_combine_body, mode=mode, final=final_g is not None, bt=bt),
        grid=(M // bt,),
        in_specs=in_specs,
        out_specs=pl.BlockSpec((bt, D), lambda i: (i, 0)),
        out_shape=jax.ShapeDtypeStruct((M, D), F32),
        scratch_shapes=[pltpu.VMEM((TOP_K, bt, D), F32), pltpu.SemaphoreType.DMA(())],
        compiler_params=_params("arbitrary"),
        name="moe_combine",
    )(*args)


def _rope_tables(pos, rope_dim):
    half = rope_dim // 2
    inv = ROPE_THETA ** (-jnp.arange(half, dtype=F32) / half)
    ang = pos.astype(F32)[:, None] * inv[None, :]
    cos, sin = jnp.cos(ang), jnp.sin(ang)
    pad = jnp.zeros((pos.shape[0], LANES - rope_dim), F32)
    return (jnp.concatenate([cos, cos, pad], axis=1), jnp.concatenate([-sin, sin, pad], axis=1))


def _pad_cols(w, n):
    return jnp.pad(w, ((0, 0), (0, n - w.shape[1])))


def kernel(x_prompt, x_sample, cache_ckv, cache_krope, state_gla, state_ssm_re, state_ssm_im,
           page_table, c_prompt, c_sample, norm_mix, norm_ffn, norm_final, w_ada, b_ada, w_in,
           gla_w_a2, gla_b_a2, gla_norm, ssm_lambda_re, ssm_lambda_im, ssm_log_dt, ssm_b_re,
           ssm_b_im, ssm_c_re, ssm_c_im, ssm_d, ssm_w_glu, ssm_b_glu, mla_q_norm, mla_kv_norm,
           mla_w_uq, mla_w_uk, mla_w_uv, w_br_gla, w_br_ssm, w_br_mla, w_out, router_w,
           router_bias, moe_w_gu, moe_w_down):
    B, T, D = x_prompt.shape
    Bd, Td = x_sample.shape[:2]
    depth = w_in.shape[0]
    _, _, H, DK, DV = state_gla.shape
    QK, VW = H * DK, H * DV
    RANK = gla_w_a2.shape[1]
    G, P = ssm_lambda_re.shape[1:]
    SW = ssm_d.shape[1]
    QL = mla_q_norm.shape[1]
    KV, MH, NOPE = mla_w_uk.shape[1:]
    ROPE = cache_krope.shape[-1]
    VD = mla_w_uv.shape[-1]
    E = router_w.shape[1]
    F = moe_w_down.shape[2]
    n_pages = page_table.shape[1]
    PAGE = cache_ckv.shape[2]
    past_len = n_pages * PAGE
    Np, Ns = B * T, Bd * Td
    n_all = Np + Ns
    Tdp = SUBLANES
    TN = 16

    o_q, o_k, o_v, o_g = 0, QK, 2 * QK, 2 * QK + VW
    o_a = o_g + VW
    o_u = o_a + RANK
    o_cq = o_u + SW
    o_ckv = o_cq + QL
    o_kr = o_ckv + KV
    o_gt = o_kr + ROPE

    cos_p, sin_p = _rope_tables(jnp.arange(T, dtype=jnp.int32), ROPE)
    cos_s, sin_s = _rope_tables(past_len + jnp.arange(Td, dtype=jnp.int32), ROPE)
    cos_s, sin_s = jnp.tile(cos_s, (Bd, 1)), jnp.tile(sin_s, (Bd, 1))

    c_all = jnp.concatenate([c_prompt, c_sample], axis=0)
    rwT = router_w.T.astype(BF16)
    rb = router_bias.astype(F32).reshape(E, 1)
    moe_bm = 256
    n_slots = ((TOP_K * n_all + E * (moe_bm - 1)) // moe_bm) * moe_bm
    wgu_b = moe_w_gu.astype(BF16)
    wd_b = moe_w_down.astype(BF16)
    cache_krT = jnp.swapaxes(cache_krope, 2, 3)

    xp = x_prompt.reshape(Np, D)
    xs = x_sample.reshape(Ns, D)
    outs = {k: [] for k in ("ckv_p", "kr_p", "ckv_s", "kr_s", "gla_p", "gla_s",
                            "sre_p", "sim_p", "sre_s", "sim_s")}
    y_p = y_s = None
    x_slots = jnp.zeros((n_slots, D), F32)
    for l in range(depth):
        wl = w_in[l]
        w_qkvg = wl[:, :o_a]
        w_misc = jnp.concatenate([wl[:, o_cq:o_kr], _pad_cols(wl[:, o_kr:o_gt], LANES),
                                  _pad_cols(wl[:, o_a:o_u], LANES)], axis=1)
        misc_w = w_misc.shape[1]
        w_u = wl[:, o_u:o_cq]
        w_gt = wl[:, o_gt:]
        alr_block = (QL + KV + LANES) // LANES
        wa = jnp.pad(gla_w_a2[l], ((0, LANES - RANK), (0, 0))).astype(BF16)
        ba = gla_b_a2[l].astype(F32).reshape(1, QK)
        ng = gla_norm[l].astype(F32).reshape(1, DV)
        wb, wc, ab = s5_weights(ssm_lambda_re[l], ssm_lambda_im[l], ssm_log_dt[l], ssm_b_re[l],
                                ssm_b_im[l], ssm_c_re[l], ssm_c_im[l])
        sd = ssm_d[l].astype(F32).reshape(1, SW)
        wglu = ssm_w_glu[l].astype(BF16)
        bglu = ssm_b_glu[l].astype(F32).reshape(1, SW)
        wq3 = mla_w_uq[l].reshape(QL, MH, NOPE + ROPE)
        wq = jnp.concatenate([wq3[:, :, :NOPE].reshape(QL, MH * NOPE),
                              jnp.pad(wq3[:, :, NOPE:], ((0, 0), (0, 0), (0, LANES - ROPE))).reshape(QL, MH * LANES)],
                             axis=1).astype(BF16)
        wuk = jnp.transpose(mla_w_uk[l], (1, 2, 0)).astype(BF16)
        wuvt = jnp.transpose(mla_w_uv[l], (1, 2, 0)).astype(BF16)
        wuv_all = mla_w_uv[l].reshape(KV, MH * VD).astype(BF16)

        mod = matmul(c_all, w_ada, layer=l, bias=b_ada[l], pre="silu", name="ada_mod")
        mod_p = mod[:B].reshape(B, 1, N_MOD * D)
        mod_s = jnp.repeat(mod[B:], Td, axis=0)

        def mixers(x, *, prompt):
            nb, nt = (B, T) if prompt else (Bd, Td)
            n = nb * nt
            rpg = T if prompt else None
            m = mod_p if prompt else mod_s
            h = norm_mod(x, norm_mix[l], m, sc_chunk=1, sh_chunk=0, rows_per_group=rpg)
            qkvg = matmul(h, w_qkvg, out_dtype=BF16, name="in_qkvg")
            misc = matmul(h, w_misc, out_dtype=F32, name="in_misc")
            u = matmul(h, w_u, out_dtype=F32, name="in_u")
            gates = matmul(h, w_gt, epilogue="sigmoid", out_dtype=BF16, name="in_gates")
            if prompt:
                o_gla, gla_s = gla_mixer(qkvg.reshape(nb, nt, -1), misc.reshape(nb, nt, -1), wa, ba, ng, None,
                                         B=nb, T=nt, H=H, DK=DK, DV=DV, C=_pick(nt, 128), t_valid=_pick(nt, 128),
                                         alr_block=alr_block, Bb=2 if nb % 2 == 0 else 1)
                o_gla = o_gla.reshape(n, VW)
            else:
                padt = ((0, 0), (0, Tdp - nt), (0, 0))
                o_gla, gla_s = gla_mixer(jnp.pad(qkvg.reshape(nb, nt, -1), padt),
                                         jnp.pad(misc.reshape(nb, nt, -1), padt), wa, ba, ng, state_gla[l],
                                         B=nb, T=Tdp, H=H, DK=DK, DV=DV, C=Tdp, t_valid=nt,
                                         alr_block=alr_block, Bb=_pick(nb, 4))
                o_gla = o_gla[:, :nt].reshape(n, VW)
            u_tm = u.reshape(nb, nt, SW).transpose(1, 0, 2).reshape(n, SW)
            if prompt:
                o_ssm, s_re, s_im = s5_mixer(u_tm, 0, wb, wc, ab, sd, wglu, bglu, None, None,
                                             R=nb, T=nt, Tc=_pick(nt, 128), W=SW)
                s_re, s_im = s_re[nb:2 * nb], s_im[nb:2 * nb]
            else:
                o_ssm, s_re, s_im = s5_mixer(u_tm, 0, wb, wc, ab, sd, wglu, bglu,
                                             state_ssm_re[l].reshape(nb, G * P), state_ssm_im[l].reshape(nb, G * P),
                                             R=nb, T=nt, Tc=_pick(nt, 2), W=SW)
            o_ssm = o_ssm.reshape(nt, nb, SW).transpose(1, 0, 2).reshape(n, SW)
            cos, sin = (cos_p, sin_p) if prompt else (cos_s, sin_s)
            qcat, kcat, ckv_n, kr_r, vt = mla_prep(misc, mla_q_norm[l], mla_kv_norm[l], wq, wuk, cos, sin,
                                                   H=MH, QL=QL, KV=KV, NOPE=NOPE, ROPE=ROPE)
            if prompt:
                o_mla = mla_attend_prompt(qcat, kcat, vt, wuvt, B=nb, T=nt, H=MH, KV=KV, VD=VD)
            else:
                q_s = qcat.reshape(MH, nb, nt, -1).transpose(1, 2, 0, 3).reshape(nb, nt * MH, -1)
                kn_s = jnp.pad(kcat.reshape(nb, nt, -1), ((0, 0), (0, TN - nt), (0, 0)))
                o_mla = mla_attend_sample(page_table, q_s, kn_s, cache_ckv, cache_krT, wuv_all, layer=l,
                                          H=MH, Td=nt, KV=KV, ROPE=ROPE, VD=VD).reshape(n, MH * VD)
            merged = merge_branches([o_gla, o_ssm, o_mla], [w_br_gla, w_br_ssm, w_br_mla], gates, layer=l)
            x = matmul(merged, w_out, layer=l, res=x, gate=m, gate_chunk=2, rows_per_group=rpg, name="out_proj")
            hf = norm_mod(x, norm_ffn[l], m, sc_chunk=4, sh_chunk=3, rows_per_group=rpg, out_dtype=F32,
                          name="norm_ffn")
            state = (ckv_n.reshape(nb, nt, KV), kr_r.reshape(nb, nt, ROPE), gla_s,
                     s_re.reshape(nb, G, P), s_im.reshape(nb, G, P))
            return x, hf, state

        xp, hf_p, st_p = mixers(xp, prompt=True)
        xs, hf_s, st_s = mixers(xs, prompt=False)
        for k, v in zip(("ckv_p", "kr_p", "gla_p", "sre_p", "sim_p"), st_p):
            outs[k].append(v)
        for k, v in zip(("ckv_s", "kr_s", "gla_s", "sre_s", "sim_s"), st_s):
            outs[k].append(v)

        idx_p, wts_p, pos_p, cnt = router(hf_p, rwT, rb, jnp.zeros((E, 1), jnp.int32))
        idx_s, wts_s, pos_s, cnt = router(hf_s, rwT, rb, cnt)
        tile_expert, tile_nvalid, (slot_p, slot_s) = moe_plan(
            cnt, [(idx_p, pos_p), (idx_s, pos_s)], E=E, bm=moe_bm, n_slots=n_slots)
        x_slots = moe_dispatch(hf_p, slot_p, x_slots)
        x_slots = moe_dispatch(hf_s, slot_s, x_slots)
        y_slots = moe_experts(x_slots, wgu_b, wd_b, tile_expert, tile_nvalid, layer=l, bm=moe_bm)
        last = l == depth - 1
        fg = norm_final if last else None
        xp = moe_combine(xp, y_slots, slot_p, wts_p.T, mod_p, gate_chunk=5, rows_per_group=T, final_g=fg)
        xs = moe_combine(xs, y_slots, slot_s, wts_s.T, mod_s, gate_chunk=5, final_g=fg)
        if last:
            y_p, y_s = xp, xs

    st = lambda k: jnp.stack(outs[k])
    return (y_p.reshape(B, T, D), y_s.reshape(Bd, Td, D),
            st("ckv_p"), st("kr_p"), st("ckv_s"), st("kr_s"), st("gla_p"), st("gla_s"),
            st("sre_p"), st("sim_p"), st("sre_s"), st("sim_s"))
```
